```python
import math
import jax, jax.numpy as jnp
from jax import lax
import numpy as np

D_MODEL = 2048
BATCH = 8
SEQ = 2048
DEPTH = 2

N_BRANCH = 4
BRANCH_WIDTH = D_MODEL // 4
ATT_HEADS = 8
ATT_HEAD_DIM = BRANCH_WIDTH // ATT_HEADS
Q_BLOCK = 128
CONF_KERNEL = 31
POOL_WINDOWS = (2, 4, 8, 16)
POOL_GROUPS = len(POOL_WINDOWS)
POOL_GROUP_DIM = BRANCH_WIDTH // POOL_GROUPS
SHORT_KERNEL = 3
D_FF = 4 * D_MODEL
RMS_EPS = 1e-6
LN_EPS = 1e-5

ATT_COLS = 3 * BRANCH_WIDTH + ATT_HEADS
CONF_COLS = 2 * BRANCH_WIDTH
POOL_COLS = BRANCH_WIDTH
SCONV_COLS = 3 * BRANCH_WIDTH
GATE_COLS = N_BRANCH * D_MODEL
IN_COLS = ATT_COLS + CONF_COLS + POOL_COLS + SCONV_COLS + GATE_COLS

kernel_name = 'hybrid_fox_conformer_pool_shortconv_block'


def rmsnorm(x, gain):
    xf = x.astype(jnp.float32)
    y = xf * lax.rsqrt(jnp.mean(xf * xf, axis=-1, keepdims=True) + RMS_EPS)
    return (y * gain.astype(jnp.float32)).astype(x.dtype)


def layernorm(x, gain, bias):
    xf = x.astype(jnp.float32)
    mu = jnp.mean(xf, axis=-1, keepdims=True)
    xc = xf - mu
    y = xc * lax.rsqrt(jnp.mean(xc * xc, axis=-1, keepdims=True) + LN_EPS)
    return (y * gain.astype(jnp.float32) + bias.astype(jnp.float32)).astype(x.dtype)


def causal_dwconv(u, w):
    K, C = w.shape
    return lax.conv_general_dilated(
        u, w[:, None, :].astype(u.dtype), window_strides=(1,), padding=[(K - 1, 0)],
        dimension_numbers=('NWC', 'WIO', 'NWC'), feature_group_count=C)


def forgetting_attention(q, k, v, f_logit, b_f, q_gain, k_gain):
    B, T, _ = q.shape
    q = rmsnorm(q.reshape(B, T, ATT_HEADS, ATT_HEAD_DIM), q_gain)
    k = rmsnorm(k.reshape(B, T, ATT_HEADS, ATT_HEAD_DIM), k_gain)
    v = v.reshape(B, T, ATT_HEADS, ATT_HEAD_DIM)
    log_f = jax.nn.log_sigmoid((f_logit + b_f).astype(jnp.float32))
    cum = jnp.cumsum(log_f, axis=1).transpose(0, 2, 1)
    scale = 1.0 / math.sqrt(ATT_HEAD_DIM)
    outs = []
    for i in range(T // Q_BLOCK):
        q0, q1 = i * Q_BLOCK, (i + 1) * Q_BLOCK
        s = jnp.einsum('bqhd,bkhd->bhqk', q[:, q0:q1], k[:, :q1]).astype(jnp.float32) * scale
        s = s + cum[:, :, q0:q1, None] - cum[:, :, None, :q1]
        mask = jnp.arange(q0, q1)[:, None] >= jnp.arange(q1)[None, :]
        s = jnp.where(mask[None, None], s, -jnp.inf)
        p = jax.nn.softmax(s, axis=-1).astype(v.dtype)
        outs.append(jnp.einsum('bhqk,bkhd->bqhd', p, v[:, :q1]))
    return jnp.concatenate(outs, axis=1).reshape(B, T, BRANCH_WIDTH)


def conformer_conv(a, g, dw, db, ln_g, ln_b):
    u = a * jax.nn.sigmoid(g)
    u = causal_dwconv(u, dw) + db
    u = layernorm(u, ln_g, ln_b)
    return jax.nn.silu(u)


def multiscale_pool(p, pool_w, pool_scale):
    B, T, C = p.shape
    pg = p.reshape(B, T, POOL_GROUPS, POOL_GROUP_DIM).astype(jnp.float32)
    cs = jnp.cumsum(pg, axis=1)
    pos = jnp.arange(T)
    pooled = []
    for g, w in enumerate(POOL_WINDOWS):
        csg = cs[:, :, g]
        shifted = jnp.pad(csg, ((0, 0), (w, 0), (0, 0)))[:, :T]
        cnt = jnp.minimum(pos + 1, w).astype(jnp.float32)[None, :, None]
        pooled.append((csg - shifted) / cnt)
    d = (jnp.stack(pooled, axis=2) - pg).astype(p.dtype)
    y = jnp.einsum('btgc,gcd->btgd', d, pool_w).reshape(B, T, C)
    return y * pool_scale


def short_gated_conv(xin, bg, cg, w):
    return bg * causal_dwconv(cg * xin, w)


def setup_inputs(seed: int = 0) -> dict:
    key = jax.random.key(seed)
    ks = jax.random.split(key, 24)
    D, W = D_MODEL, BRANCH_WIDTH
    nrm = lambda k, shape, s: jax.random.normal(k, shape, jnp.float32) * s
    return {
        'x': nrm(ks[0], (BATCH, SEQ, D), 1.0),
        'c': nrm(ks[1], (BATCH, D), 1.0),
        'w_ada': nrm(ks[2], (DEPTH, D, 6 * D), 0.5 * D ** -0.5),
        'b_ada': nrm(ks[3], (DEPTH, 6 * D), 0.01),
        'norm_gain': 1.0 + nrm(ks[4], (DEPTH, 2, D), 0.02),
        'w_in': nrm(ks[5], (DEPTH, D, IN_COLS), D ** -0.5),
        'b_f': jax.random.uniform(ks[6], (DEPTH, ATT_HEADS), jnp.float32, 1.0, 6.0),
        'b_gate': nrm(ks[7], (DEPTH, GATE_COLS), 0.01),
        'q_gain': 1.0 + nrm(ks[8], (DEPTH, ATT_HEAD_DIM), 0.02),
        'k_gain': 1.0 + nrm(ks[9], (DEPTH, ATT_HEAD_DIM), 0.02),
        'conf_dw': nrm(ks[10], (DEPTH, CONF_KERNEL, W), CONF_KERNEL ** -0.5),
        'conf_db': nrm(ks[11], (DEPTH, W), 0.01),
        'conf_ln_g': 1.0 + nrm(ks[12], (DEPTH, W), 0.02),
        'conf_ln_b': nrm(ks[13], (DEPTH, W), 0.01),
        'pool_w': nrm(ks[14], (DEPTH, POOL_GROUPS, POOL_GROUP_DIM, POOL_GROUP_DIM), POOL_GROUP_DIM ** -0.5),
        'pool_scale': 1.0 + nrm(ks[15], (DEPTH, W), 0.02),
        'sconv_w': nrm(ks[16], (DEPTH, SHORT_KERNEL, W), SHORT_KERNEL ** -0.5),
        'w_branch': nrm(ks[17], (DEPTH, N_BRANCH, W, D), W ** -0.5),
        'w_out': nrm(ks[18], (DEPTH, D, D), D ** -0.5),
        'w_mlp1': nrm(ks[19], (DEPTH, D, D_FF), D ** -0.5),
        'w_mlp2': nrm(ks[20], (DEPTH, D_FF, D), D_FF ** -0.5),
    }


def reference(x, c, w_ada, b_ada, norm_gain, w_in, b_f, b_gate, q_gain, k_gain,
              conf_dw, conf_db, conf_ln_g, conf_ln_b, pool_w, pool_scale, sconv_w,
              w_branch, w_out, w_mlp1, w_mlp2):
    B, T, D = x.shape
    W = BRANCH_WIDTH
    for l in range(DEPTH):
        mod = (c @ w_ada[l] + b_ada[l]).astype(x.dtype)
        sh_m, sc_m, g_m, sh_f, sc_f, g_f = jnp.split(mod[:, None, :], 6, axis=-1)

        h = rmsnorm(x, norm_gain[l, 0]) * (1.0 + sc_m) + sh_m
        z = h @ w_in[l]
        o = 0
        zq, zk, zv = z[..., o:o + W], z[..., o + W:o + 2 * W], z[..., o + 2 * W:o + 3 * W]
        zf = z[..., o + 3 * W:o + ATT_COLS]; o += ATT_COLS
        za, zg = z[..., o:o + W], z[..., o + W:o + 2 * W]; o += CONF_COLS
        zp = z[..., o:o + W]; o += POOL_COLS
        zx, zb, zc = z[..., o:o + W], z[..., o + W:o + 2 * W], z[..., o + 2 * W:o + 3 * W]; o += SCONV_COLS
        gates = jax.nn.sigmoid(z[..., o:o + GATE_COLS] + b_gate[l]).reshape(B, T, N_BRANCH, D)

        y_att = forgetting_attention(zq, zk, zv, zf, b_f[l], q_gain[l], k_gain[l])
        y_conf = conformer_conv(za, zg, conf_dw[l], conf_db[l], conf_ln_g[l], conf_ln_b[l])
        y_pool = multiscale_pool(zp, pool_w[l], pool_scale[l])
        y_sconv = short_gated_conv(zx, zb, zc, sconv_w[l])

        merged = (gates[:, :, 0] * (y_att @ w_branch[l, 0])
                  + gates[:, :, 1] * (y_conf @ w_branch[l, 1])
                  + gates[:, :, 2] * (y_pool @ w_branch[l, 2])
                  + gates[:, :, 3] * (y_sconv @ w_branch[l, 3]))
        x = x + g_m * (merged @ w_out[l])

        h = rmsnorm(x, norm_gain[l, 1]) * (1.0 + sc_f) + sh_f
        x = x + g_f * (jnp.square(jax.nn.relu(h @ w_mlp1[l])) @ w_mlp2[l])
    return x
```

```python
import functools
import math

import jax
import jax.numpy as jnp
from jax import lax
from jax.experimental import pallas as pl
from jax.experimental.pallas import tpu as pltpu

F32 = jnp.float32
BF16 = jnp.bfloat16

D_MODEL = 2048
WIDTH = D_MODEL // 4
HEADS = 8
HEAD_DIM = WIDTH // HEADS
CONF_K = 31
POOL_WINDOWS = (2, 4, 8, 16)
POOL_GROUP = WIDTH // len(POOL_WINDOWS)
SHORT_K = 3
D_FF = 4 * D_MODEL
RMS_EPS = 1e-6
LN_EPS = 1e-5

LANES = 128
SUBLANES = 8
VMEM_LIMIT = 56 * 1024 * 1024

ADA_TN = 1024
INPROJ_TM = 1024
INPROJ_TN = 1536
ATT_BQ = 256
HEADS_PER_STEP = LANES // HEAD_DIM
CONV_TT = 512
CONV_HALO = 32
CONV_ROWS = 32
MERGE_TM = 512
MERGE_TN = 256
MLP_TM = 1024
MLP_TF = 512

MIX_COLS = 9 * WIDTH


def _params(*sem):
    return pltpu.CompilerParams(dimension_semantics=sem, vmem_limit_bytes=VMEM_LIMIT)


def _sigmoid(x):
    return 1.0 / (1.0 + jnp.exp(-x))


def _modulated_rmsnorm(x, gain, shift, scale):
    y = x * lax.rsqrt(jnp.mean(x * x, axis=-1, keepdims=True) + RMS_EPS) * gain
    return y * (1.0 + scale) + shift


def _ada_kernel(c_ref, w_ref, b_ref, o_ref):
    o_ref[...] = jnp.dot(c_ref[...].astype(BF16), w_ref[...].astype(BF16),
                         preferred_element_type=F32) + b_ref[...]


def _ada(c, w_ada, b_ada):
    depth, d, n = w_ada.shape
    b = c.shape[0]
    return pl.pallas_call(
        _ada_kernel,
        grid=(depth, n // ADA_TN),
        in_specs=[
            pl.BlockSpec((b, d), lambda l, j: (0, 0)),
            pl.BlockSpec((None, d, ADA_TN), lambda l, j: (l, 0, j)),
            pl.BlockSpec((None, 1, ADA_TN), lambda l, j: (l, 0, j)),
        ],
        out_specs=pl.BlockSpec((None, b, ADA_TN), lambda l, j: (l, 0, j)),
        out_shape=jax.ShapeDtypeStruct((depth, b, n), F32),
        compiler_params=_params("arbitrary", "arbitrary"),
        name="ada",
    )(c, w_ada, b_ada.reshape(depth, 1, n))


def _inproj_kernel(x_ref, mod_ref, gain_ref, w_ref, wf_ref, z_ref, zf_ref, h_ref):
    @pl.when(pl.program_id(1) == 0)
    def _():
        h = _modulated_rmsnorm(x_ref[...], gain_ref[...], mod_ref[0:1, :], mod_ref[1:2, :])
        hb = h.astype(BF16)
        h_ref[...] = hb
        zf_ref[...] = jnp.dot(hb, wf_ref[...], preferred_element_type=F32)

    z_ref[...] = jnp.dot(h_ref[...], w_ref[...], preferred_element_type=F32).astype(z_ref.dtype)


def _inproj(x2, mod, gain, w_mix, w_f, seq):
    n, d = x2.shape
    tm, tn = INPROJ_TM, INPROJ_TN
    per_batch = seq // tm
    return pl.pallas_call(
        _inproj_kernel,
        grid=(n // tm, MIX_COLS // tn),
        in_specs=[
            pl.BlockSpec((tm, d), lambda i, j: (i, 0)),
            pl.BlockSpec((None, 6, d), lambda i, j: (i // per_batch, 0, 0)),
            pl.BlockSpec((1, d), lambda i, j: (0, 0)),
            pl.BlockSpec((d, tn), lambda i, j: (0, j)),
            pl.BlockSpec((d, LANES), lambda i, j: (0, 0)),
        ],
        out_specs=[
            pl.BlockSpec((tm, tn), lambda i, j: (i, j)),
            pl.BlockSpec((tm, LANES), lambda i, j: (i, 0)),
        ],
        out_shape=[
            jax.ShapeDtypeStruct((n, MIX_COLS), BF16),
            jax.ShapeDtypeStruct((n, LANES), F32),
        ],
        scratch_shapes=[pltpu.VMEM((tm, d), BF16)],
        compiler_params=_params("arbitrary", "arbitrary"),
        name="inproj",
    )(x2, mod, gain, w_mix, w_f)


def _attn_kernel(q_ref, k_ref, v_ref, zf_ref, bf_ref, qg_ref, kg_ref, o_ref,
                 cum_row_ref, cum_col_ref, kn_ref):
    seq = q_ref.shape[0]
    g = pl.program_id(1)

    @pl.when(g == 0)
    def _():
        logit = zf_ref[...] + bf_ref[...]
        log_f = jnp.minimum(logit, 0.0) - jnp.log1p(jnp.exp(-jnp.abs(logit)))
        acc = log_f.T
        lane = lax.broadcasted_iota(jnp.int32, acc.shape, 1)
        shift = 1
        while shift < seq:
            acc = acc + jnp.where(lane >= shift, pltpu.roll(acc, shift, axis=1), 0.0)
            shift *= 2
        cum_row_ref[...] = acc
        cum_col_ref[...] = acc.T

    scale = 1.0 / math.sqrt(HEAD_DIM)
    nblk = seq // ATT_BQ
    row = lax.broadcasted_iota(jnp.int32, (ATT_BQ, ATT_BQ), 0)
    col = lax.broadcasted_iota(jnp.int32, (ATT_BQ, ATT_BQ), 1)
    causal = row >= col
    head_lane = lax.broadcasted_iota(jnp.int32, (ATT_BQ, LANES), 1)

    for hh in range(HEADS_PER_STEP):
        lo = hh * HEAD_DIM
        k = k_ref[:, lo:lo + HEAD_DIM].astype(F32)
        kn = k * lax.rsqrt(jnp.mean(k * k, axis=-1, keepdims=True) + RMS_EPS) * kg_ref[...]
        kn_ref[hh] = kn.astype(BF16)

    nt = (((1,), (1,)), ((), ()))
    for i in range(nblk):
        q0, q1 = i * ATT_BQ, (i + 1) * ATT_BQ
        outs = []
        for hh in range(HEADS_PER_STEP):
            lo = hh * HEAD_DIM
            head = g * HEADS_PER_STEP + hh
            q = q_ref[q0:q1, lo:lo + HEAD_DIM].astype(F32)
            qn = q * lax.rsqrt(jnp.mean(q * q, axis=-1, keepdims=True) + RMS_EPS) * qg_ref[...]
            qn = (qn * scale).astype(BF16)
            cum_t = jnp.sum(jnp.where(head_lane == head, cum_col_ref[q0:q1, :], 0.0),
                            axis=1, keepdims=True)
            ref_c = cum_t[ATT_BQ - 1:ATT_BQ, :]
            cum_t = cum_t - ref_c
            cum_s = cum_row_ref[pl.ds(head, 1), 0:q1] - ref_c

            s_d = lax.dot_general(qn, kn_ref[hh, q0:q1, :], nt, preferred_element_type=F32)
            s_d = (s_d + cum_t) - cum_s[:, q0:q1]
            s_d = jnp.where(causal, s_d, -jnp.inf)
            m = jnp.max(s_d, axis=1, keepdims=True)
            if i > 0:
                s_o = lax.dot_general(qn, kn_ref[hh, 0:q0, :], nt, preferred_element_type=F32)
                s_o = (s_o + cum_t) - cum_s[:, 0:q0]
                m = jnp.maximum(m, jnp.max(s_o, axis=1, keepdims=True))
            p_d = jnp.exp(s_d - m)
            den = jnp.sum(p_d, axis=1, keepdims=True)
            acc = jnp.dot(p_d.astype(BF16), v_ref[q0:q1, lo:lo + HEAD_DIM],
                          preferred_element_type=F32)
            if i > 0:
                p_o = jnp.exp(s_o - m)
                den = den + jnp.sum(p_o, axis=1, keepdims=True)
                acc = acc + jnp.dot(p_o.astype(BF16), v_ref[0:q0, lo:lo + HEAD_DIM],
                                    preferred_element_type=F32)
            outs.append(acc / den)
        o_ref[q0:q1, :] = jnp.concatenate(outs, axis=1).astype(o_ref.dtype)


def _attention(z, zf, b_f_row, q_gain, k_gain, batch, seq):
    n = z.shape[0]
    groups = HEADS // HEADS_PER_STEP
    blk = lambda off: pl.BlockSpec((seq, LANES), lambda b, g, off=off: (b, off + g))
    return pl.pallas_call(
        _attn_kernel,
        grid=(batch, groups),
        in_specs=[
            blk(0), blk(groups), blk(2 * groups),
            pl.BlockSpec((seq, LANES), lambda b, g: (b, 0)),
            pl.BlockSpec((1, LANES), lambda b, g: (0, 0)),
            pl.BlockSpec((1, HEAD_DIM), lambda b, g: (0, 0)),
            pl.BlockSpec((1, HEAD_DIM), lambda b, g: (0, 0)),
        ],
        out_specs=pl.BlockSpec((seq, LANES), lambda b, g: (b, g)),
        out_shape=jax.ShapeDtypeStruct((n, WIDTH), BF16),
        scratch_shapes=[
            pltpu.VMEM((LANES, seq), F32),
            pltpu.VMEM((seq, LANES), F32),
            pltpu.VMEM((HEADS_PER_STEP, seq, HEAD_DIM), BF16),
        ],
        compiler_params=_params("arbitrary", "arbitrary"),
        name="attn",
    )(z, z, z, zf, b_f_row, q_gain, k_gain)


def _conv_kernel(a_ref, g_ref, p_ref, x_ref, b_ref, c_ref,
                 ah_ref, gh_ref, ph_ref, xh_ref, ch_ref,
                 dw_ref, db_ref, lng_ref, lnb_ref, pw_ref, ps_ref, sw_ref,
                 o_ref, ubuf, pbuf, vbuf):
    tt = a_ref.shape[0]
    t = pl.program_id(1)
    has_past = t > 0
    H = CONV_HALO

    u_h = ah_ref[...].astype(F32) * _sigmoid(gh_ref[...].astype(F32))
    ubuf[0:H, :] = jnp.where(has_past, u_h, 0.0)
    ubuf[H:, :] = a_ref[...].astype(F32) * _sigmoid(g_ref[...].astype(F32))
    pbuf[0:H, :] = jnp.where(has_past, ph_ref[...].astype(F32), 0.0)
    pbuf[H:, :] = p_ref[...].astype(F32)
    v_h = ch_ref[...].astype(F32) * xh_ref[...].astype(F32)
    vbuf[0:H, :] = jnp.where(has_past, v_h, 0.0)
    vbuf[H:, :] = c_ref[...].astype(F32) * x_ref[...].astype(F32)

    R = CONV_ROWS

    def chunk(ci, carry):
        r0 = pl.multiple_of(ci * R, R)

        acc = jnp.broadcast_to(db_ref[...], (R, WIDTH))
        win = ubuf[pl.ds(r0, R + H), :]
        for r in range(SUBLANES):
            rolled = win if r == 0 else pltpu.roll(win, r, axis=0)
            for a in range((CONF_K - 1 - r) // SUBLANES + 1):
                k = CONF_K - 1 - (SUBLANES * a + r)
                lo = H - SUBLANES * a
                acc = acc + dw_ref[k:k + 1, :] * rolled[lo:lo + R, :]
        mu = jnp.mean(acc, axis=-1, keepdims=True)
        xc = acc - mu
        y = xc * lax.rsqrt(jnp.mean(xc * xc, axis=-1, keepdims=True) + LN_EPS)
        y = y * lng_ref[...] + lnb_ref[...]
        o_ref[pl.ds(r0, R), 0:WIDTH] = (y * _sigmoid(y)).astype(o_ref.dtype)

        pos = t * tt + r0 + lax.broadcasted_iota(jnp.int32, (R, 1), 0)
        pooled = []
        for gi, w in enumerate(POOL_WINDOWS):
            lo = gi * POOL_GROUP
            tot = pbuf[pl.ds(r0, R + H), lo:lo + POOL_GROUP]
            tok = tot[H:H + R, :]
            span = 1
            while span < w:
                tot = tot + pltpu.roll(tot, span, axis=0)
                span *= 2
            cnt = jnp.minimum(pos + 1, w).astype(F32)
            dlt = (tot[H:H + R, :] / cnt - tok).astype(BF16)
            pooled.append(jnp.dot(dlt, pw_ref[gi], preferred_element_type=F32))
        yp = jnp.concatenate(pooled, axis=1) * ps_ref[...]
        o_ref[pl.ds(r0, R), WIDTH:2 * WIDTH] = yp.astype(o_ref.dtype)

        vwin = vbuf[pl.ds(r0, R + H), :]
        sc = sw_ref[SHORT_K - 1:SHORT_K, :] * vwin[H:H + R, :]
        for dly in range(1, SHORT_K):
            k = SHORT_K - 1 - dly
            sc = sc + sw_ref[k:k + 1, :] * pltpu.roll(vwin, dly, axis=0)[H:H + R, :]
        ys = b_ref[pl.ds(r0, R), :].astype(F32) * sc
        o_ref[pl.ds(r0, R), 2 * WIDTH:3 * WIDTH] = ys.astype(o_ref.dtype)
        return carry

    lax.fori_loop(0, tt // R, chunk, 0)


def _conv_mixers(z, conf_dw, conf_db, ln_g, ln_b, pool_w, pool_scale, sconv_w, batch, seq):
    n = z.shape[0]
    tt, H = CONV_TT, CONV_HALO
    tiles = seq // tt
    main = lambda cb: pl.BlockSpec((tt, WIDTH), lambda b, t, cb=cb: (b * tiles + t, cb))
    halo = lambda cb: pl.BlockSpec(
        (H, WIDTH),
        lambda b, t, cb=cb: (jnp.maximum((b * tiles + t) * (tt // H) - 1, 0), cb))
    full = lambda shape: pl.BlockSpec(shape, lambda b, t: (0,) * len(shape))
    return pl.pallas_call(
        _conv_kernel,
        grid=(batch, tiles),
        in_specs=[
            main(3), main(4), main(5), main(6), main(7), main(8),
            halo(3), halo(4), halo(5), halo(6), halo(8),
            full((CONF_K, WIDTH)), full((1, WIDTH)), full((1, WIDTH)), full((1, WIDTH)),
            full((len(POOL_WINDOWS), POOL_GROUP, POOL_GROUP)), full((1, WIDTH)),
            full((SHORT_K, WIDTH)),
        ],
        out_specs=pl.BlockSpec((tt, 3 * WIDTH), lambda b, t: (b * tiles + t, 0)),
        out_shape=jax.ShapeDtypeStruct((n, 3 * WIDTH), BF16),
        scratch_shapes=[pltpu.VMEM((H + tt, WIDTH), F32)] * 3,
        compiler_params=_params("arbitrary", "arbitrary"),
        name="conv",
    )(z, z, z, z, z, z, z, z, z, z, z,
      conf_dw, conf_db, ln_g, ln_b, pool_w, pool_scale, sconv_w)


def _merge_kernel(x_ref, mod_ref, gain_ref, ya_ref, yr_ref,
                  wg0_ref, wg1_ref, wg2_ref, wg3_ref, bg0_ref, bg1_ref, bg2_ref, bg3_ref,
                  wbr_ref, wout_ref, o_ref, h_ref):
    j = pl.program_id(1)

    @pl.when(j == 0)
    def _():
        h = _modulated_rmsnorm(x_ref[...], gain_ref[...], mod_ref[0:1, :], mod_ref[1:2, :])
        h_ref[...] = h.astype(BF16)
        o_ref[...] = jnp.zeros_like(o_ref)

    h = h_ref[...]
    wgs = (wg0_ref, wg1_ref, wg2_ref, wg3_ref)
    bgs = (bg0_ref, bg1_ref, bg2_ref, bg3_ref)
    merged = None
    for b in range(4):
        y = ya_ref[...] if b == 0 else yr_ref[:, (b - 1) * WIDTH:b * WIDTH]
        gate = _sigmoid(jnp.dot(h, wgs[b][...], preferred_element_type=F32) + bgs[b][...])
        term = gate * jnp.dot(y, wbr_ref[b], preferred_element_type=F32)
        merged = term if merged is None else merged + term
    o_ref[...] += jnp.dot(merged.astype(BF16), wout_ref[...], preferred_element_type=F32)

    @pl.when(j == pl.num_programs(1) - 1)
    def _():
        o_ref[...] = x_ref[...] + mod_ref[2:3, :] * o_ref[...]


def _merge(x2, mod, gain, y_att, y_rest, w_gate, b_gate, w_branch, w_out, seq):
    n, d = x2.shape
    tm, tn = MERGE_TM, MERGE_TN
    per_batch = seq // tm
    nj = d // tn
    wg = lambda b: pl.BlockSpec((d, tn), lambda i, j, b=b: (0, b * nj + j))
    bg = lambda b: pl.BlockSpec((1, tn), lambda i, j, b=b: (0, b * nj + j))
    return pl.pallas_call(
        _merge_kernel,
        grid=(n // tm, nj),
        in_specs=[
            pl.BlockSpec((tm, d), lambda i, j: (i, 0)),
            pl.BlockSpec((None, 6, d), lambda i, j: (i // per_batch, 0, 0)),
            pl.BlockSpec((1, d), lambda i, j: (0, 0)),
            pl.BlockSpec((tm, WIDTH), lambda i, j: (i, 0)),
            pl.BlockSpec((tm, 3 * WIDTH), lambda i, j: (i, 0)),
            wg(0), wg(1), wg(2), wg(3), bg(0), bg(1), bg(2), bg(3),
            pl.BlockSpec((4, WIDTH, tn), lambda i, j: (0, 0, j)),
            pl.BlockSpec((tn, d), lambda i, j: (j, 0)),
        ],
        out_specs=pl.BlockSpec((tm, d), lambda i, j: (i, 0)),
        out_shape=jax.ShapeDtypeStruct((n, d), F32),
        scratch_shapes=[pltpu.VMEM((tm, d), BF16)],
        compiler_params=_params("arbitrary", "arbitrary"),
        name="merge",
    )(x2, mod, gain, y_att, y_rest, w_gate, w_gate, w_gate, w_gate,
      b_gate, b_gate, b_gate, b_gate, w_branch, w_out)


def _mlp_kernel(x_ref, mod_ref, gain_ref, w1_ref, w2_ref, o_ref, h_ref):
    j = pl.program_id(1)

    @pl.when(j == 0)
    def _():
        h = _modulated_rmsnorm(x_ref[...], gain_ref[...], mod_ref[3:4, :], mod_ref[4:5, :])
        h_ref[...] = h.astype(BF16)
        o_ref[...] = jnp.zeros_like(o_ref)

    hid = jnp.maximum(jnp.dot(h_ref[...], w1_ref[...], preferred_element_type=F32), 0.0)
    o_ref[...] += jnp.dot((hid * hid).astype(BF16), w2_ref[...], preferred_element_type=F32)

    @pl.when(j == pl.num_programs(1) - 1)
    def _():
        o_ref[...] = x_ref[...] + mod_ref[5:6, :] * o_ref[...]


def _mlp(x2, mod, gain, w1, w2, seq):
    n, d = x2.shape
    tm, tf = MLP_TM, MLP_TF
    per_batch = seq // tm
    return pl.pallas_call(
        _mlp_kernel,
        grid=(n // tm, D_FF // tf),
        in_specs=[
            pl.BlockSpec((tm, d), lambda i, j: (i, 0)),
            pl.BlockSpec((None, 6, d), lambda i, j: (i // per_batch, 0, 0)),
            pl.BlockSpec((1, d), lambda i, j: (0, 0)),
            pl.BlockSpec((d, tf), lambda i, j: (0, j)),
            pl.BlockSpec((tf, d), lambda i, j: (j, 0)),
        ],
        out_specs=pl.BlockSpec((tm, d), lambda i, j: (i, 0)),
        out_shape=jax.ShapeDtypeStruct((n, d), F32),
        scratch_shapes=[pltpu.VMEM((tm, d), BF16)],
        compiler_params=_params("arbitrary", "arbitrary"),
        name="mlp",
    )(x2, mod, gain, w1, w2)


def kernel(x, c, w_ada, b_ada, norm_gain, w_in, b_f, b_gate, q_gain, k_gain, conf_dw, conf_db,
           conf_ln_g, conf_ln_b, pool_w, pool_scale, sconv_w, w_branch, w_out, w_mlp1, w_mlp2):
    batch, seq, d = x.shape
    depth = w_ada.shape[0]
    W = WIDTH
    att_cols = 3 * W + HEADS

    mod_all = _ada(c, w_ada, b_ada).reshape(depth, batch, 6, d)
    x2 = x.reshape(batch * seq, d)
    for l in range(depth):
        mod = mod_all[l]
        wl = w_in[l]
        w_mix = jnp.concatenate([wl[:, :3 * W], wl[:, att_cols:att_cols + 6 * W]], axis=1).astype(BF16)
        w_f = jnp.pad(wl[:, 3 * W:att_cols], ((0, 0), (0, LANES - HEADS))).astype(BF16)
        w_gate = wl[:, att_cols + 6 * W:].astype(BF16)
        b_f_row = jnp.pad(b_f[l], (0, LANES - HEADS)).reshape(1, LANES)

        z, zf = _inproj(x2, mod, norm_gain[l, 0:1], w_mix, w_f, seq)
        y_att = _attention(z, zf, b_f_row, q_gain[l].reshape(1, HEAD_DIM),
                           k_gain[l].reshape(1, HEAD_DIM), batch, seq)
        y_rest = _conv_mixers(z, conf_dw[l], conf_db[l].reshape(1, W), conf_ln_g[l].reshape(1, W),
                              conf_ln_b[l].reshape(1, W), pool_w[l].astype(BF16),
                              pool_scale[l].reshape(1, W), sconv_w[l], batch, seq)
        x2 = _merge(x2, mod, norm_gain[l, 0:1], y_att, y_rest, w_gate,
                    b_gate[l].reshape(1, 4 * d), w_branch[l].astype(BF16), w_out[l].astype(BF16), seq)
        x2 = _mlp(x2, mod, norm_gain[l, 1:2], w_mlp1[l].astype(BF16), w_mlp2[l].astype(BF16), seq)
    return x2.reshape(batch, seq, d)
```

```python
import functools
import math

import jax
import jax.numpy as jnp
from jax import lax
from jax.experimental import pallas as pl
from jax.experimental.pallas import tpu as pltpu

F32 = jnp.float32
BF16 = jnp.bfloat16

D_MODEL = 2048
WIDTH = D_MODEL // 4
HEADS = 8
HEAD_DIM = WIDTH // HEADS
CONF_K = 31
POOL_WINDOWS = (2, 4, 8, 16)
POOL_GROUP = WIDTH // len(POOL_WINDOWS)
SHORT_K = 3
D_FF = 4 * D_MODEL
RMS_EPS = 1e-6
LN_EPS = 1e-5

LANES = 128
SUBLANES = 8
VMEM_LIMIT = 56 * 1024 * 1024

ADA_TK = 128
INPROJ_TM = 1024
INPROJ_TN = 1536
ATT_BQ = 256
HEADS_PER_STEP = LANES // HEAD_DIM
ATT_AUG = 3
LOG2E = math.log2(math.e)
CONV_TT = 512
CONV_HALO = 32
CONV_ROWS = 32
MERGE_TM = 512
MERGE_TN = 256
MLP_TM = 1024
MLP_TF = 512

MIX_COLS = 9 * WIDTH


def _params(*sem):
    return pltpu.CompilerParams(dimension_semantics=sem, vmem_limit_bytes=VMEM_LIMIT)


def _sigmoid(x):
    return 1.0 / (1.0 + jnp.exp(-x))


def _modulated_rmsnorm(x, gain, shift, scale):
    y = x * lax.rsqrt(jnp.mean(x * x, axis=-1, keepdims=True) + RMS_EPS) * gain
    return y * (1.0 + scale) + shift


def _ada_kernel(c_ref, w_ref, b_ref, o_ref):
    @pl.when(pl.program_id(1) == 0)
    def _():
        o_ref[...] = jnp.broadcast_to(b_ref[...], o_ref.shape)

    o_ref[...] += jnp.dot(c_ref[...].astype(BF16), w_ref[...].astype(BF16),
                          preferred_element_type=F32)


def _ada(c, w_ada, b_ada):
    depth, d, n = w_ada.shape
    b = c.shape[0]
    return pl.pallas_call(
        _ada_kernel,
        grid=(depth, d // ADA_TK),
        in_specs=[
            pl.BlockSpec((b, ADA_TK), lambda l, k: (0, k)),
            pl.BlockSpec((None, ADA_TK, n), lambda l, k: (l, k, 0)),
            pl.BlockSpec((None, 1, n), lambda l, k: (l, 0, 0)),
        ],
        out_specs=pl.BlockSpec((None, b, n), lambda l, k: (l, 0, 0)),
        out_shape=jax.ShapeDtypeStruct((depth, b, n), F32),
        compiler_params=_params("arbitrary", "arbitrary"),
        name="ada",
    )(c, w_ada, b_ada.reshape(depth, 1, n))


def _inproj_kernel(x_ref, mod_ref, gain_ref, w_ref, wf_ref, z_ref, zf_ref, h_ref):
    @pl.when(pl.program_id(1) == 0)
    def _():
        h = _modulated_rmsnorm(x_ref[...], gain_ref[...], mod_ref[0:1, :], mod_ref[1:2, :])
        hb = h.astype(BF16)
        h_ref[...] = hb
        zf_ref[...] = jnp.dot(hb, wf_ref[...], preferred_element_type=F32)

    z_ref[...] = jnp.dot(h_ref[...], w_ref[...], preferred_element_type=F32).astype(z_ref.dtype)


def _inproj(x2, mod, gain, w_mix, w_f, seq):
    n, d = x2.shape
    tm, tn = INPROJ_TM, INPROJ_TN
    per_batch = seq // tm
    return pl.pallas_call(
        _inproj_kernel,
        grid=(n // tm, MIX_COLS // tn),
        in_specs=[
            pl.BlockSpec((tm, d), lambda i, j: (i, 0)),
            pl.BlockSpec((None, 6, d), lambda i, j: (i // per_batch, 0, 0)),
            pl.BlockSpec((1, d), lambda i, j: (0, 0)),
            pl.BlockSpec((d, tn), lambda i, j: (0, j)),
            pl.BlockSpec((d, LANES), lambda i, j: (0, 0)),
        ],
        out_specs=[
            pl.BlockSpec((tm, tn), lambda i, j: (i, j)),
            pl.BlockSpec((tm, LANES), lambda i, j: (i, 0)),
            pl.BlockSpec((tm, d), lambda i, j: (i, 0)),
        ],
        out_shape=[
            jax.ShapeDtypeStruct((n, MIX_COLS), BF16),
            jax.ShapeDtypeStruct((n, LANES), F32),
            jax.ShapeDtypeStruct((n, d), BF16),
        ],
        compiler_params=_params("arbitrary", "arbitrary"),
        name="inproj",
    )(x2, mod, gain, w_mix, w_f)


def _attn_kernel(q_ref, k_ref, v_ref, zf_ref, bf_ref, qg_ref, kg_ref, o_ref,
                 cum_ref, qa_ref, ka_ref, s_ref, p_ref):
    seq = q_ref.shape[0]
    g = pl.program_id(1)

    @pl.when(g == 0)
    def _():
        logit = zf_ref[...] + bf_ref[...]
        log_f = jnp.minimum(logit, 0.0) - jnp.log1p(jnp.exp(-jnp.abs(logit)))
        acc = log_f.T[0:HEADS, :]
        lane = lax.broadcasted_iota(jnp.int32, acc.shape, 1)
        shift = 1
        while shift < seq:
            acc = acc + jnp.where(lane >= shift, pltpu.roll(acc, shift, axis=1), 0.0)
            shift *= 2
        pad = jnp.zeros((LANES - HEADS, seq), F32)
        rest = jnp.concatenate([acc * LOG2E, pad], axis=0).T
        for j in range(ATT_AUG):
            piece = rest.astype(BF16)
            cum_ref[:, j * LANES:(j + 1) * LANES] = piece
            rest = rest - piece.astype(F32)

    scale = 1.0 / math.sqrt(HEAD_DIM)
    lane = lax.broadcasted_iota(jnp.int32, (seq, LANES), 1)
    in_h0 = lane < HEAD_DIM

    def normalise(ref, gain_ref, mult):
        x = ref[...].astype(F32)
        sq = x * x
        ms0 = jnp.sum(jnp.where(in_h0, sq, 0.0), axis=1, keepdims=True)
        ms1 = jnp.sum(jnp.where(in_h0, 0.0, sq), axis=1, keepdims=True)
        inv = lax.rsqrt(jnp.where(in_h0, ms0, ms1) * (1.0 / HEAD_DIM) + RMS_EPS)
        return x * inv * (gain_ref[...] * mult)

    qn = normalise(q_ref, qg_ref, scale * LOG2E)
    kn = normalise(k_ref, kg_ref, 1.0)

    e_row = lax.broadcasted_iota(jnp.int32, (LANES, LANES), 0)
    e_col = lax.broadcasted_iota(jnp.int32, (LANES, LANES), 1)
    for hh in range(HEADS_PER_STEP):
        head = g * HEADS_PER_STEP + hh
        base = HEAD_DIM * (1 - hh)
        pick = e_row == head
        sel = []
        for j in range(ATT_AUG):
            plus = jnp.where(pick & (e_col == base + j), 1.0, 0.0)
            minus = jnp.where(pick & (e_col == base + ATT_AUG + j), 1.0, 0.0)
            sel.append(plus - minus)
        sel = jnp.concatenate(sel, axis=0).astype(BF16)
        aug = jnp.dot(cum_ref[...], sel, preferred_element_type=F32)
        in_head = in_h0 if hh == 0 else jnp.logical_not(in_h0)
        first = (lane >= base) & (lane < base + ATT_AUG)
        second = (lane >= base + ATT_AUG) & (lane < base + 2 * ATT_AUG)
        qa = jnp.where(in_head, qn, jnp.where(first, aug, jnp.where(second, 1.0, 0.0)))
        ka = jnp.where(in_head, kn, jnp.where(first, 1.0, jnp.where(second, aug, 0.0)))
        qa_ref[hh] = qa.astype(BF16)
        ka_ref[hh] = ka.astype(BF16)

    bq = ATT_BQ
    row = lax.broadcasted_iota(jnp.int32, (bq, bq), 0)
    col = lax.broadcasted_iota(jnp.int32, (bq, bq), 1)
    causal = row >= col
    out_h0 = lax.broadcasted_iota(jnp.int32, (bq, LANES), 1) < HEAD_DIM
    nt = (((1,), (1,)), ((), ()))
    for i in range(seq // bq):
        q0 = i * bq
        outs = []
        for hh in range(HEADS_PER_STEP):
            slot = (i * HEADS_PER_STEP + hh) % 2
            qa = qa_ref[hh, q0:q0 + bq, :]
            m_part = None
            for c in range(i + 1):
                k0 = c * bq
                s = lax.dot_general(qa, ka_ref[hh, k0:k0 + bq, :], nt, preferred_element_type=F32)
                if c == i:
                    s = jnp.where(causal, s, -jnp.inf)
                s_ref[slot, :, k0:k0 + bq] = s
                for h0 in range(0, bq, LANES):
                    part = s[:, h0:h0 + LANES]
                    m_part = part if m_part is None else jnp.maximum(m_part, part)
            m = jnp.max(m_part, axis=1, keepdims=True)
            l_part = None
            for c in range(i + 1):
                k0 = c * bq
                p = jnp.exp2(s_ref[slot, :, k0:k0 + bq] - m)
                for h0 in range(0, bq, LANES):
                    part = p[:, h0:h0 + LANES]
                    l_part = part if l_part is None else l_part + part
                p_ref[slot, :, k0:k0 + bq] = p.astype(BF16)
            acc = jnp.dot(p_ref[slot, :, 0:q0 + bq], v_ref[0:q0 + bq, :],
                          preferred_element_type=F32)
            outs.append(acc / jnp.sum(l_part, axis=1, keepdims=True))
        o_ref[q0:q0 + bq, :] = jnp.where(out_h0, outs[0], outs[1]).astype(o_ref.dtype)


def _attention(z, zf, b_f_row, q_gain, k_gain, batch, seq):
    n = z.shape[0]
    groups = HEADS // HEADS_PER_STEP
    blk = lambda off: pl.BlockSpec((seq, LANES), lambda b, g, off=off: (b, off + g))
    return pl.pallas_call(
        _attn_kernel,
        grid=(batch, groups),
        in_specs=[
            blk(0), blk(groups), blk(2 * groups),
            pl.BlockSpec((seq, LANES), lambda b, g: (b, 0)),
            pl.BlockSpec((1, LANES), lambda b, g: (0, 0)),
            pl.BlockSpec((1, LANES), lambda b, g: (0, 0)),
            pl.BlockSpec((1, LANES), lambda b, g: (0, 0)),
        ],
        out_specs=pl.BlockSpec((seq, LANES), lambda b, g: (b, g)),
        out_shape=jax.ShapeDtypeStruct((n, WIDTH), BF16),
        scratch_shapes=[
            pltpu.VMEM((seq, ATT_AUG * LANES), BF16),
            pltpu.VMEM((HEADS_PER_STEP, seq, LANES), BF16),
            pltpu.VMEM((HEADS_PER_STEP, seq, LANES), BF16),
            pltpu.VMEM((2, ATT_BQ, seq), F32),
            pltpu.VMEM((2, ATT_BQ, seq), BF16),
        ],
        compiler_params=_params("arbitrary", "arbitrary"),
        name="attn",
    )(z, z, z, zf, b_f_row, q_gain, k_gain)


def _conv_kernel(a_ref, g_ref, p_ref, x_ref, b_ref, c_ref,
                 ah_ref, gh_ref, ph_ref, xh_ref, ch_ref,
                 dw_ref, db_ref, lng_ref, lnb_ref, pw_ref, ps_ref, sw_ref,
                 o_ref, ubuf, pbuf, vbuf):
    tt = a_ref.shape[0]
    t = pl.program_id(1)
    has_past = t > 0
    H = CONV_HALO

    u_h = ah_ref[...].astype(F32) * _sigmoid(gh_ref[...].astype(F32))
    ubuf[0:H, :] = jnp.where(has_past, u_h, 0.0)
    ubuf[H:, :] = a_ref[...].astype(F32) * _sigmoid(g_ref[...].astype(F32))
    pbuf[0:H, :] = jnp.where(has_past, ph_ref[...].astype(F32), 0.0)
    pbuf[H:, :] = p_ref[...].astype(F32)
    v_h = ch_ref[...].astype(F32) * xh_ref[...].astype(F32)
    vbuf[0:H, :] = jnp.where(has_past, v_h, 0.0)
    vbuf[H:, :] = c_ref[...].astype(F32) * x_ref[...].astype(F32)

    R = CONV_ROWS

    def chunk(ci, carry):
        r0 = pl.multiple_of(ci * R, R)

        acc = jnp.broadcast_to(db_ref[...], (R, WIDTH))
        win = ubuf[pl.ds(r0, R + H), :]
        for r in range(SUBLANES):
            rolled = win if r == 0 else pltpu.roll(win, r, axis=0)
            for a in range((CONF_K - 1 - r) // SUBLANES + 1):
                k = CONF_K - 1 - (SUBLANES * a + r)
                lo = H - SUBLANES * a
                acc = acc + dw_ref[k:k + 1, :] * rolled[lo:lo + R, :]
        mu = jnp.mean(acc, axis=-1, keepdims=True)
        xc = acc - mu
        y = xc * lax.rsqrt(jnp.mean(xc * xc, axis=-1, keepdims=True) + LN_EPS)
        y = y * lng_ref[...] + lnb_ref[...]
        o_ref[pl.ds(r0, R), 0:WIDTH] = (y * _sigmoid(y)).astype(o_ref.dtype)

        pos = t * tt + r0 + lax.broadcasted_iota(jnp.int32, (R, 1), 0)
        pooled = []
        for gi, w in enumerate(POOL_WINDOWS):
            lo = gi * POOL_GROUP
            tot = pbuf[pl.ds(r0, R + H), lo:lo + POOL_GROUP]
            tok = tot[H:H + R, :]
            span = 1
            while span < w:
                tot = tot + pltpu.roll(tot, span, axis=0)
                span *= 2
            cnt = jnp.minimum(pos + 1, w).astype(F32)
            dlt = (tot[H:H + R, :] / cnt - tok).astype(BF16)
            pooled.append(jnp.dot(dlt, pw_ref[gi], preferred_element_type=F32))
        yp = jnp.concatenate(pooled, axis=1) * ps_ref[...]
        o_ref[pl.ds(r0, R), WIDTH:2 * WIDTH] = yp.astype(o_ref.dtype)

        vwin = vbuf[pl.ds(r0, R + H), :]
        sc = sw_ref[SHORT_K - 1:SHORT_K, :] * vwin[H:H + R, :]
        for dly in range(1, SHORT_K):
            k = SHORT_K - 1 - dly
            sc = sc + sw_ref[k:k + 1, :] * pltpu.roll(vwin, dly, axis=0)[H:H + R, :]
        ys = b_ref[pl.ds(r0, R), :].astype(F32) * sc
        o_ref[pl.ds(r0, R), 2 * WIDTH:3 * WIDTH] = ys.astype(o_ref.dtype)
        return carry

    lax.fori_loop(0, tt // R, chunk, 0)


def _conv_mixers(z, conf_dw, conf_db, ln_g, ln_b, pool_w, pool_scale, sconv_w, batch, seq):
    n = z.shape[0]
    tt, H = CONV_TT, CONV_HALO
    tiles = seq // tt
    main = lambda cb: pl.BlockSpec((tt, WIDTH), lambda b, t, cb=cb: (b * tiles + t, cb))
    halo = lambda cb: pl.BlockSpec(
        (H, WIDTH),
        lambda b, t, cb=cb: (jnp.maximum((b * tiles + t) * (tt // H) - 1, 0), cb))
    full = lambda shape: pl.BlockSpec(shape, lambda b, t: (0,) * len(shape))
    return pl.pallas_call(
        _conv_kernel,
        grid=(batch, tiles),
        in_specs=[
            main(3), main(4), main(5), main(6), main(7), main(8),
            halo(3), halo(4), halo(5), halo(6), halo(8),
            full((CONF_K, WIDTH)), full((1, WIDTH)), full((1, WIDTH)), full((1, WIDTH)),
            full((len(POOL_WINDOWS), POOL_GROUP, POOL_GROUP)), full((1, WIDTH)),
            full((SHORT_K, WIDTH)),
        ],
        out_specs=pl.BlockSpec((tt, 3 * WIDTH), lambda b, t: (b * tiles + t, 0)),
        out_shape=jax.ShapeDtypeStruct((n, 3 * WIDTH), BF16),
        scratch_shapes=[pltpu.VMEM((H + tt, WIDTH), F32)] * 3,
        compiler_params=_params("arbitrary", "arbitrary"),
        name="conv",
    )(z, z, z, z, z, z, z, z, z, z, z,
      conf_dw, conf_db, ln_g, ln_b, pool_w, pool_scale, sconv_w)


def _merge_kernel(x_ref, mod_ref, h_ref, ya_ref, yr_ref,
                  wg0_ref, wg1_ref, wg2_ref, wg3_ref, bg0_ref, bg1_ref, bg2_ref, bg3_ref,
                  wbr_ref, wout_ref, o_ref):
    j = pl.program_id(1)

    @pl.when(j == 0)
    def _():
        o_ref[...] = jnp.zeros_like(o_ref)

    h = h_ref[...]
    wgs = (wg0_ref, wg1_ref, wg2_ref, wg3_ref)
    bgs = (bg0_ref, bg1_ref, bg2_ref, bg3_ref)
    merged = None
    for b in range(4):
        y = ya_ref[...] if b == 0 else yr_ref[:, (b - 1) * WIDTH:b * WIDTH]
        gate = _sigmoid(jnp.dot(h, wgs[b][...], preferred_element_type=F32) + bgs[b][...])
        term = gate * jnp.dot(y, wbr_ref[b], preferred_element_type=F32)
        merged = term if merged is None else merged + term
    o_ref[...] += jnp.dot(merged.astype(BF16), wout_ref[...], preferred_element_type=F32)

    @pl.when(j == pl.num_programs(1) - 1)
    def _():
        o_ref[...] = x_ref[...] + mod_ref[2:3, :] * o_ref[...]


def _merge(x2, mod, h, y_att, y_rest, w_gate, b_gate, w_branch, w_out, seq):
    n, d = x2.shape
    tm, tn = MERGE_TM, MERGE_TN
    per_batch = seq // tm
    nj = d // tn
    wg = lambda b: pl.BlockSpec((d, tn), lambda i, j, b=b: (0, b * nj + j))
    bg = lambda b: pl.BlockSpec((1, tn), lambda i, j, b=b: (0, b * nj + j))
    return pl.pallas_call(
        _merge_kernel,
        grid=(n // tm, nj),
        in_specs=[
            pl.BlockSpec((tm, d), lambda i, j: (i, 0)),
            pl.BlockSpec((None, 6, d), lambda i, j: (i // per_batch, 0, 0)),
            pl.BlockSpec((tm, d), lambda i, j: (i, 0)),
            pl.BlockSpec((tm, WIDTH), lambda i, j: (i, 0)),
            pl.BlockSpec((tm, 3 * WIDTH), lambda i, j: (i, 0)),
            wg(0), wg(1), wg(2), wg(3), bg(0), bg(1), bg(2), bg(3),
            pl.BlockSpec((4, WIDTH, tn), lambda i, j: (0, 0, j)),
            pl.BlockSpec((tn, d), lambda i, j: (j, 0)),
        ],
        out_specs=pl.BlockSpec((tm, d), lambda i, j: (i, 0)),
        out_shape=jax.ShapeDtypeStruct((n, d), F32),
        compiler_params=_params("arbitrary", "arbitrary"),
        name="merge",
    )(x2, mod, h, y_att, y_rest, w_gate, w_gate, w_gate, w_gate,
      b_gate, b_gate, b_gate, b_gate, w_branch, w_out)


def _mlp_kernel(x_ref, mod_ref, gain_ref, w1_ref, w2_ref, o_ref, h_ref):
    j = pl.program_id(1)

    @pl.when(j == 0)
    def _():
        h = _modulated_rmsnorm(x_ref[...], gain_ref[...], mod_ref[3:4, :], mod_ref[4:5, :])
        h_ref[...] = h.astype(BF16)
        o_ref[...] = jnp.zeros_like(o_ref)

    hid = jnp.maximum(jnp.dot(h_ref[...], w1_ref[...], preferred_element_type=F32), 0.0)
    o_ref[...] += jnp.dot((hid * hid).astype(BF16), w2_ref[...], preferred_element_type=F32)

    @pl.when(j == pl.num_programs(1) - 1)
    def _():
        o_ref[...] = x_ref[...] + mod_ref[5:6, :] * o_ref[...]


def _mlp(x2, mod, gain, w1, w2, seq):
    n, d = x2.shape
    tm, tf = MLP_TM, MLP_TF
    per_batch = seq // tm
    return pl.pallas_call(
        _mlp_kernel,
        grid=(n // tm, D_FF // tf),
        in_specs=[
            pl.BlockSpec((tm, d), lambda i, j: (i, 0)),
            pl.BlockSpec((None, 6, d), lambda i, j: (i // per_batch, 0, 0)),
            pl.BlockSpec((1, d), lambda i, j: (0, 0)),
            pl.BlockSpec((d, tf), lambda i, j: (0, j)),
            pl.BlockSpec((tf, d), lambda i, j: (j, 0)),
        ],
        out_specs=pl.BlockSpec((tm, d), lambda i, j: (i, 0)),
        out_shape=jax.ShapeDtypeStruct((n, d), F32),
        scratch_shapes=[pltpu.VMEM((tm, d), BF16)],
        compiler_params=_params("arbitrary", "arbitrary"),
        name="mlp",
    )(x2, mod, gain, w1, w2)


def kernel(x, c, w_ada, b_ada, norm_gain, w_in, b_f, b_gate, q_gain, k_gain, conf_dw, conf_db,
           conf_ln_g, conf_ln_b, pool_w, pool_scale, sconv_w, w_branch, w_out, w_mlp1, w_mlp2):
    batch, seq, d = x.shape
    depth = w_ada.shape[0]
    W = WIDTH
    att_cols = 3 * W + HEADS

    mod_all = _ada(c, w_ada, b_ada).reshape(depth, batch, 6, d)
    x2 = x.reshape(batch * seq, d)
    for l in range(depth):
        mod = mod_all[l]
        wl = w_in[l]
        w_mix = jnp.concatenate([wl[:, :3 * W], wl[:, att_cols:att_cols + 6 * W]], axis=1).astype(BF16)
        w_f = jnp.pad(wl[:, 3 * W:att_cols], ((0, 0), (0, LANES - HEADS))).astype(BF16)
        w_gate = wl[:, att_cols + 6 * W:].astype(BF16)
        b_f_row = jnp.pad(b_f[l], (0, LANES - HEADS)).reshape(1, LANES)

        z, zf, h = _inproj(x2, mod, norm_gain[l, 0:1], w_mix, w_f, seq)
        y_att = _attention(z, zf, b_f_row, jnp.tile(q_gain[l], HEADS_PER_STEP).reshape(1, LANES),
                           jnp.tile(k_gain[l], HEADS_PER_STEP).reshape(1, LANES), batch, seq)
        y_rest = _conv_mixers(z, conf_dw[l], conf_db[l].reshape(1, W), conf_ln_g[l].reshape(1, W),
                              conf_ln_b[l].reshape(1, W), pool_w[l].astype(BF16),
                              pool_scale[l].reshape(1, W), sconv_w[l], batch, seq)
        x2 = _merge(x2, mod, h, y_att, y_rest, w_gate,
                    b_gate[l].reshape(1, 4 * d), w_branch[l].astype(BF16), w_out[l].astype(BF16), seq)
        x2 = _mlp(x2, mod, norm_gain[l, 1:2], w_mlp1[l].astype(BF16), w_mlp2[l].astype(BF16), seq)
    return x2.reshape(batch, seq, d)
```

```python
import functools
import math

import jax
import jax.numpy as jnp
from jax import lax
from jax.experimental import pallas as pl
from jax.experimental.pallas import tpu as pltpu

F32 = jnp.float32
BF16 = jnp.bfloat16

D_MODEL = 2048
WIDTH = D_MODEL // 4
HEADS = 8
HEAD_DIM = WIDTH // HEADS
CONF_K = 31
POOL_WINDOWS = (2, 4, 8, 16)
POOL_GROUP = WIDTH // len(POOL_WINDOWS)
SHORT_K = 3
D_FF = 4 * D_MODEL
RMS_EPS = 1e-6
LN_EPS = 1e-5

LANES = 128
SUBLANES = 8
VMEM_LIMIT = 56 * 1024 * 1024

ADA_TK = 128
ADA_SPLIT = 4
INPROJ_TM = 1024
INPROJ_TN = 1536
ATT_BQ = 256
HEADS_PER_STEP = LANES // HEAD_DIM
ATT_AUG = 3
LOG2E = math.log2(math.e)
CONV_TT = 512
CONV_HALO = 32
CONV_ROWS = 32
POOL_HIST = 16
MERGE_TM = 1024
MERGE_TN = 256
MERGE_TO = 512
MLP_TM = 1024
MLP_TF = 512

MIX_COLS = 9 * WIDTH
GATE_COL0 = MIX_COLS
FORGET_COL0 = MIX_COLS + 4 * D_MODEL
PACKED_COLS = FORGET_COL0 + LANES


def _params(*sem):
    return pltpu.CompilerParams(dimension_semantics=sem, vmem_limit_bytes=VMEM_LIMIT)


def _sigmoid(x):
    return 1.0 / (1.0 + jnp.exp(-x))


def _modulated_rmsnorm(x, gain, shift, scale):
    y = x * lax.rsqrt(jnp.mean(x * x, axis=-1, keepdims=True) + RMS_EPS) * gain
    return y * (1.0 + scale) + shift


def _ada_kernel(c_ref, *refs):
    w_refs, b_ref, o_ref = refs[:ADA_SPLIT], refs[ADA_SPLIT], refs[ADA_SPLIT + 1]

    @pl.when(pl.program_id(1) == 0)
    def _():
        o_ref[...] = jnp.broadcast_to(b_ref[...], o_ref.shape)

    cb = c_ref[...].astype(BF16)
    part = o_ref.shape[1] // ADA_SPLIT
    for s, w_ref in enumerate(w_refs):
        o_ref[:, s * part:(s + 1) * part] += jnp.dot(cb, w_ref[...].astype(BF16),
                                                     preferred_element_type=F32)


def _ada(c, w_ada, b_ada):
    depth, d, n = w_ada.shape
    b = c.shape[0]
    part = n // ADA_SPLIT
    w_spec = lambda s: pl.BlockSpec((None, ADA_TK, part), lambda l, k, s=s: (l, k, s))
    return pl.pallas_call(
        _ada_kernel,
        grid=(depth, d // ADA_TK),
        in_specs=[pl.BlockSpec((b, ADA_TK), lambda l, k: (0, k))]
        + [w_spec(s) for s in range(ADA_SPLIT)]
        + [pl.BlockSpec((None, 1, n), lambda l, k: (l, 0, 0))],
        out_specs=pl.BlockSpec((None, b, n), lambda l, k: (l, 0, 0)),
        out_shape=jax.ShapeDtypeStruct((depth, b, n), F32),
        compiler_params=_params("arbitrary", "arbitrary"),
        name="ada",
    )(c, *([w_ada] * ADA_SPLIT), b_ada.reshape(depth, 1, n))


def _inproj_kernel(x_ref, mod_ref, gain_ref, w_ref, wf_ref, z_ref, zf_ref, h_ref):
    @pl.when(pl.program_id(1) == 0)
    def _():
        h = _modulated_rmsnorm(x_ref[...], gain_ref[...], mod_ref[0:1, :], mod_ref[1:2, :])
        hb = h.astype(BF16)
        h_ref[...] = hb
        zf_ref[...] = jnp.dot(hb, wf_ref[...], preferred_element_type=F32)

    z_ref[...] = jnp.dot(h_ref[...], w_ref[...], preferred_element_type=F32).astype(z_ref.dtype)


def _inproj(x2, mod, gain, w_packed, layer, seq):
    n, d = x2.shape
    tm, tn = INPROJ_TM, INPROJ_TN
    per_batch = seq // tm
    return pl.pallas_call(
        _inproj_kernel,
        grid=(n // tm, MIX_COLS // tn),
        in_specs=[
            pl.BlockSpec((tm, d), lambda i, j: (i, 0)),
            pl.BlockSpec((None, 6, d), lambda i, j: (i // per_batch, 0, 0)),
            pl.BlockSpec((1, d), lambda i, j: (0, 0)),
            pl.BlockSpec((None, d, tn), lambda i, j: (layer, 0, j)),
            pl.BlockSpec((None, d, LANES), lambda i, j: (layer, 0, FORGET_COL0 // LANES)),
        ],
        out_specs=[
            pl.BlockSpec((tm, tn), lambda i, j: (i, j)),
            pl.BlockSpec((tm, LANES), lambda i, j: (i, 0)),
            pl.BlockSpec((tm, d), lambda i, j: (i, 0)),
        ],
        out_shape=[
            jax.ShapeDtypeStruct((n, MIX_COLS), BF16),
            jax.ShapeDtypeStruct((n, LANES), F32),
            jax.ShapeDtypeStruct((n, d), BF16),
        ],
        compiler_params=_params("arbitrary", "arbitrary"),
        name="inproj",
    )(x2, mod, gain, w_packed, w_packed)


def _attn_kernel(q_ref, k_ref, v_ref, zf_ref, bf_ref, qg_ref, kg_ref, o_ref,
                 cum_ref, qa_ref, ka_ref, s_ref, p_ref):
    seq = q_ref.shape[0]
    g = pl.program_id(1)

    @pl.when(g == 0)
    def _():
        logit = zf_ref[...] + bf_ref[...]
        log_f = jnp.minimum(logit, 0.0) - jnp.log1p(jnp.exp(-jnp.abs(logit)))
        acc = log_f.T[0:HEADS, :]
        lane = lax.broadcasted_iota(jnp.int32, acc.shape, 1)
        shift = 1
        while shift < seq:
            acc = acc + jnp.where(lane >= shift, pltpu.roll(acc, shift, axis=1), 0.0)
            shift *= 2
        pad = jnp.zeros((LANES - HEADS, seq), F32)
        rest = jnp.concatenate([acc * LOG2E, pad], axis=0).T
        for j in range(ATT_AUG):
            piece = rest.astype(BF16)
            cum_ref[:, j * LANES:(j + 1) * LANES] = piece
            rest = rest - piece.astype(F32)

    scale = 1.0 / math.sqrt(HEAD_DIM)
    lane = lax.broadcasted_iota(jnp.int32, (seq, LANES), 1)
    in_h0 = lane < HEAD_DIM

    def normalise(ref, gain_ref, mult):
        x = ref[...].astype(F32)
        sq = x * x
        ms0 = jnp.sum(jnp.where(in_h0, sq, 0.0), axis=1, keepdims=True)
        ms1 = jnp.sum(jnp.where(in_h0, 0.0, sq), axis=1, keepdims=True)
        inv = lax.rsqrt(jnp.where(in_h0, ms0, ms1) * (1.0 / HEAD_DIM) + RMS_EPS)
        return x * inv * (gain_ref[...] * mult)

    qn = normalise(q_ref, qg_ref, scale * LOG2E)
    kn = normalise(k_ref, kg_ref, 1.0)

    e_row = lax.broadcasted_iota(jnp.int32, (LANES, LANES), 0)
    e_col = lax.broadcasted_iota(jnp.int32, (LANES, LANES), 1)
    for hh in range(HEADS_PER_STEP):
        head = g * HEADS_PER_STEP + hh
        base = HEAD_DIM * (1 - hh)
        pick = e_row == head
        sel = []
        for j in range(ATT_AUG):
            plus = jnp.where(pick & (e_col == base + j), 1.0, 0.0)
            minus = jnp.where(pick & (e_col == base + ATT_AUG + j), 1.0, 0.0)
            sel.append(plus - minus)
        sel = jnp.concatenate(sel, axis=0).astype(BF16)
        aug = jnp.dot(cum_ref[...], sel, preferred_element_type=F32)
        in_head = in_h0 if hh == 0 else jnp.logical_not(in_h0)
        first = (lane >= base) & (lane < base + ATT_AUG)
        second = (lane >= base + ATT_AUG) & (lane < base + 2 * ATT_AUG)
        qa = jnp.where(in_head, qn, jnp.where(first, aug, jnp.where(second, 1.0, 0.0)))
        ka = jnp.where(in_head, kn, jnp.where(first, 1.0, jnp.where(second, aug, 0.0)))
        qa_ref[hh] = qa.astype(BF16)
        ka_ref[hh] = ka.astype(BF16)

    bq = ATT_BQ
    row = lax.broadcasted_iota(jnp.int32, (bq, bq), 0)
    col = lax.broadcasted_iota(jnp.int32, (bq, bq), 1)
    causal = row >= col
    out_h0 = lax.broadcasted_iota(jnp.int32, (bq, LANES), 1) < HEAD_DIM
    nt = (((1,), (1,)), ((), ()))
    for i in range(seq // bq):
        q0 = i * bq
        outs = []
        for hh in range(HEADS_PER_STEP):
            slot = (i * HEADS_PER_STEP + hh) % 2
            qa = qa_ref[hh, q0:q0 + bq, :]
            m_part = None
            for c in range(i + 1):
                k0 = c * bq
                s = lax.dot_general(qa, ka_ref[hh, k0:k0 + bq, :], nt, preferred_element_type=F32)
                if c == i:
                    s = jnp.where(causal, s, -jnp.inf)
                s_ref[slot, :, k0:k0 + bq] = s
                for h0 in range(0, bq, LANES):
                    part = s[:, h0:h0 + LANES]
                    m_part = part if m_part is None else jnp.maximum(m_part, part)
            m = jnp.max(m_part, axis=1, keepdims=True)
            l_part = None
            for c in range(i + 1):
                k0 = c * bq
                p = jnp.exp2(s_ref[slot, :, k0:k0 + bq] - m)
                for h0 in range(0, bq, LANES):
                    part = p[:, h0:h0 + LANES]
                    l_part = part if l_part is None else l_part + part
                p_ref[slot, :, k0:k0 + bq] = p.astype(BF16)
            acc = jnp.dot(p_ref[slot, :, 0:q0 + bq], v_ref[0:q0 + bq, :],
                          preferred_element_type=F32)
            outs.append(acc / jnp.sum(l_part, axis=1, keepdims=True))
        o_ref[q0:q0 + bq, :] = jnp.where(out_h0, outs[0], outs[1]).astype(o_ref.dtype)


def _attention(z, zf, b_f_row, q_gain, k_gain, batch, seq):
    n = z.shape[0]
    groups = HEADS // HEADS_PER_STEP
    blk = lambda off: pl.BlockSpec((seq, LANES), lambda b, g, off=off: (b, off + g))
    return pl.pallas_call(
        _attn_kernel,
        grid=(batch, groups),
        in_specs=[
            blk(0), blk(groups), blk(2 * groups),
            pl.BlockSpec((seq, LANES), lambda b, g: (b, 0)),
            pl.BlockSpec((1, LANES), lambda b, g: (0, 0)),
            pl.BlockSpec((1, LANES), lambda b, g: (0, 0)),
            pl.BlockSpec((1, LANES), lambda b, g: (0, 0)),
        ],
        out_specs=pl.BlockSpec((seq, LANES), lambda b, g: (b, g)),
        out_shape=jax.ShapeDtypeStruct((n, WIDTH), BF16),
        scratch_shapes=[
            pltpu.VMEM((seq, ATT_AUG * LANES), BF16),
            pltpu.VMEM((HEADS_PER_STEP, seq, LANES), BF16),
            pltpu.VMEM((HEADS_PER_STEP, seq, LANES), BF16),
            pltpu.VMEM((2, ATT_BQ, seq), F32),
            pltpu.VMEM((2, ATT_BQ, seq), BF16),
        ],
        compiler_params=_params("arbitrary", "arbitrary"),
        name="attn",
    )(z, z, z, zf, b_f_row, q_gain, k_gain)


def _conv_kernel(a_ref, g_ref, p_ref, x_ref, b_ref, c_ref,
                 ah_ref, gh_ref, ph_ref, xh_ref, ch_ref,
                 dw_ref, db_ref, lng_ref, lnb_ref, pw_ref, ps_ref, sw_ref,
                 o_ref, ubuf, pbuf, vbuf):
    tt = a_ref.shape[0]
    t = pl.program_id(1)
    has_past = t > 0
    H = CONV_HALO

    u_h = ah_ref[...].astype(F32) * _sigmoid(gh_ref[...].astype(F32))
    ubuf[0:H, :] = jnp.where(has_past, u_h, 0.0)
    ubuf[H:, :] = a_ref[...].astype(F32) * _sigmoid(g_ref[...].astype(F32))
    pbuf[0:H, :] = jnp.where(has_past, ph_ref[...].astype(F32), 0.0)
    pbuf[H:, :] = p_ref[...].astype(F32)
    v_h = ch_ref[...].astype(F32) * xh_ref[...].astype(F32)
    vbuf[0:H, :] = jnp.where(has_past, v_h, 0.0)
    vbuf[H:, :] = c_ref[...].astype(F32) * x_ref[...].astype(F32)

    R = CONV_ROWS

    def chunk(ci, carry):
        r0 = pl.multiple_of(ci * R, R)

        acc = jnp.broadcast_to(db_ref[...], (R, WIDTH))
        win = ubuf[pl.ds(r0, R + H), :]
        for r in range(SUBLANES):
            rolled = win if r == 0 else pltpu.roll(win, r, axis=0)
            for a in range((CONF_K - 1 - r) // SUBLANES + 1):
                k = CONF_K - 1 - (SUBLANES * a + r)
                lo = H - SUBLANES * a
                acc = acc + dw_ref[k:k + 1, :] * rolled[lo:lo + R, :]
        mu = jnp.mean(acc, axis=-1, keepdims=True)
        xc = acc - mu
        y = xc * lax.rsqrt(jnp.mean(xc * xc, axis=-1, keepdims=True) + LN_EPS)
        y = y * lng_ref[...] + lnb_ref[...]
        o_ref[pl.ds(r0, R), 0:WIDTH] = (y * _sigmoid(y)).astype(o_ref.dtype)

        pos = t * tt + r0 + lax.broadcasted_iota(jnp.int32, (R, 1), 0)
        pooled = []
        for gi, w in enumerate(POOL_WINDOWS):
            lo = gi * POOL_GROUP
            tot = pbuf[pl.ds(r0 + (H - POOL_HIST), R + POOL_HIST), lo:lo + POOL_GROUP]
            tok = tot[POOL_HIST:, :]
            span = 1
            while span < w:
                tot = tot + pltpu.roll(tot, span, axis=0)
                span *= 2
            cnt = jnp.minimum(pos + 1, w).astype(F32)
            dlt = (tot[POOL_HIST:, :] / cnt - tok).astype(BF16)
            pooled.append(jnp.dot(dlt, pw_ref[gi], preferred_element_type=F32))
        yp = jnp.concatenate(pooled, axis=1) * ps_ref[...]
        o_ref[pl.ds(r0, R), WIDTH:2 * WIDTH] = yp.astype(o_ref.dtype)

        vwin = vbuf[pl.ds(r0 + (H - SUBLANES), R + SUBLANES), :]
        sc = sw_ref[SHORT_K - 1:SHORT_K, :] * vwin[SUBLANES:, :]
        for dly in range(1, SHORT_K):
            k = SHORT_K - 1 - dly
            sc = sc + sw_ref[k:k + 1, :] * pltpu.roll(vwin, dly, axis=0)[SUBLANES:, :]
        ys = b_ref[pl.ds(r0, R), :].astype(F32) * sc
        o_ref[pl.ds(r0, R), 2 * WIDTH:3 * WIDTH] = ys.astype(o_ref.dtype)
        return carry

    lax.fori_loop(0, tt // R, chunk, 0, unroll=2)


def _conv_mixers(z, conf_dw, conf_db, ln_g, ln_b, pool_w, layer, pool_scale, sconv_w, batch, seq):
    n = z.shape[0]
    tt, H = CONV_TT, CONV_HALO
    tiles = seq // tt
    main = lambda cb: pl.BlockSpec((tt, WIDTH), lambda b, t, cb=cb: (b * tiles + t, cb))
    halo = lambda cb: pl.BlockSpec(
        (H, WIDTH),
        lambda b, t, cb=cb: (jnp.maximum((b * tiles + t) * (tt // H) - 1, 0), cb))
    full = lambda shape: pl.BlockSpec(shape, lambda b, t: (0,) * len(shape))
    return pl.pallas_call(
        _conv_kernel,
        grid=(batch, tiles),
        in_specs=[
            main(3), main(4), main(5), main(6), main(7), main(8),
            halo(3), halo(4), halo(5), halo(6), halo(8),
            full((CONF_K, WIDTH)), full((1, WIDTH)), full((1, WIDTH)), full((1, WIDTH)),
            pl.BlockSpec((None, len(POOL_WINDOWS), POOL_GROUP, POOL_GROUP),
                         lambda b, t: (layer, 0, 0, 0)),
            full((1, WIDTH)),
            full((SHORT_K, WIDTH)),
        ],
        out_specs=pl.BlockSpec((tt, 3 * WIDTH), lambda b, t: (b * tiles + t, 0)),
        out_shape=jax.ShapeDtypeStruct((n, 3 * WIDTH), BF16),
        scratch_shapes=[pltpu.VMEM((H + tt, WIDTH), F32)] * 3,
        compiler_params=_params("arbitrary", "arbitrary"),
        name="conv",
    )(z, z, z, z, z, z, z, z, z, z, z,
      conf_dw, conf_db, ln_g, ln_b, pool_w, pool_scale, sconv_w)


def _merge_kernel(x_ref, mod_ref, h_ref, ya_ref, yr_ref,
                  wg0_ref, wg1_ref, wg2_ref, wg3_ref, bg0_ref, bg1_ref, bg2_ref, bg3_ref,
                  wbr_ref, wout_ref, o_ref, m_ref):
    j = pl.program_id(1)
    na = m_ref.shape[0]

    @pl.when(j < na)
    def _():
        h = h_ref[...]
        wgs = (wg0_ref, wg1_ref, wg2_ref, wg3_ref)
        bgs = (bg0_ref, bg1_ref, bg2_ref, bg3_ref)
        merged = None
        for b in range(4):
            y = ya_ref[...] if b == 0 else yr_ref[:, (b - 1) * WIDTH:b * WIDTH]
            gate = _sigmoid(jnp.dot(h, wgs[b][...], preferred_element_type=F32) + bgs[b][...])
            term = gate * jnp.dot(y, wbr_ref[b], preferred_element_type=F32)
            merged = term if merged is None else merged + term
        m_ref[j] = merged.astype(BF16)

    @pl.when(j >= na)
    def _():
        merged = jnp.concatenate([m_ref[k] for k in range(na)], axis=1)
        proj = jnp.dot(merged, wout_ref[...], preferred_element_type=F32)
        o_ref[...] = x_ref[...] + mod_ref[2:3, :] * proj


def _merge(x2, mod, h, y_att, y_rest, w_packed, b_gate, w_branch, w_out, layer, seq):
    n, d = x2.shape
    tm, tn, to = MERGE_TM, MERGE_TN, MERGE_TO
    per_batch = seq // tm
    na, nb = d // tn, d // to
    ja = lambda j: jnp.minimum(j, na - 1)
    jb = lambda j: jnp.maximum(j - na, 0)
    gate0 = GATE_COL0 // tn
    wg = lambda b: pl.BlockSpec((None, d, tn), lambda i, j, b=b: (layer, 0, gate0 + b * na + ja(j)))
    bg = lambda b: pl.BlockSpec((None, 1, tn), lambda i, j, b=b: (layer, 0, b * na + ja(j)))
    return pl.pallas_call(
        _merge_kernel,
        grid=(n // tm, na + nb),
        in_specs=[
            pl.BlockSpec((tm, to), lambda i, j: (i, jb(j))),
            pl.BlockSpec((None, 6, to), lambda i, j: (i // per_batch, 0, jb(j))),
            pl.BlockSpec((tm, d), lambda i, j: (i, 0)),
            pl.BlockSpec((tm, WIDTH), lambda i, j: (i, 0)),
            pl.BlockSpec((tm, 3 * WIDTH), lambda i, j: (i, 0)),
            wg(0), wg(1), wg(2), wg(3), bg(0), bg(1), bg(2), bg(3),
            pl.BlockSpec((None, 4, WIDTH, tn), lambda i, j: (layer, 0, 0, ja(j))),
            pl.BlockSpec((None, d, to), lambda i, j: (layer, 0, jb(j))),
        ],
        out_specs=pl.BlockSpec((tm, to), lambda i, j: (i, jb(j))),
        out_shape=jax.ShapeDtypeStruct((n, d), F32),
        scratch_shapes=[pltpu.VMEM((na, tm, tn), BF16)],
        compiler_params=_params("arbitrary", "arbitrary"),
        name="merge",
    )(x2, mod, h, y_att, y_rest, w_packed, w_packed, w_packed, w_packed,
      b_gate, b_gate, b_gate, b_gate, w_branch, w_out)


def _mlp_kernel(x_ref, mod_ref, gain_ref, w1_ref, w2_ref, o_ref, h_ref):
    j = pl.program_id(1)

    @pl.when(j == 0)
    def _():
        h = _modulated_rmsnorm(x_ref[...], gain_ref[...], mod_ref[3:4, :], mod_ref[4:5, :])
        h_ref[...] = h.astype(BF16)
        o_ref[...] = jnp.zeros_like(o_ref)

    hid = jnp.maximum(jnp.dot(h_ref[...], w1_ref[...], preferred_element_type=F32), 0.0)
    o_ref[...] += jnp.dot((hid * hid).astype(BF16), w2_ref[...], preferred_element_type=F32)

    @pl.when(j == pl.num_programs(1) - 1)
    def _():
        o_ref[...] = x_ref[...] + mod_ref[5:6, :] * o_ref[...]


def _mlp(x2, mod, gain, w1, w2, layer, seq):
    n, d = x2.shape
    tm, tf = MLP_TM, MLP_TF
    per_batch = seq // tm
    return pl.pallas_call(
        _mlp_kernel,
        grid=(n // tm, D_FF // tf),
        in_specs=[
            pl.BlockSpec((tm, d), lambda i, j: (i, 0)),
            pl.BlockSpec((None, 6, d), lambda i, j: (i // per_batch, 0, 0)),
            pl.BlockSpec((1, d), lambda i, j: (0, 0)),
            pl.BlockSpec((None, d, tf), lambda i, j: (layer, 0, j)),
            pl.BlockSpec((None, tf, d), lambda i, j: (layer, j, 0)),
        ],
        out_specs=pl.BlockSpec((tm, d), lambda i, j: (i, 0)),
        out_shape=jax.ShapeDtypeStruct((n, d), F32),
        scratch_shapes=[pltpu.VMEM((tm, d), BF16)],
        compiler_params=_params("arbitrary", "arbitrary"),
        name="mlp",
    )(x2, mod, gain, w1, w2)


def kernel(x, c, w_ada, b_ada, norm_gain, w_in, b_f, b_gate, q_gain, k_gain, conf_dw, conf_db,
           conf_ln_g, conf_ln_b, pool_w, pool_scale, sconv_w, w_branch, w_out, w_mlp1, w_mlp2):
    batch, seq, d = x.shape
    depth = w_ada.shape[0]
    W = WIDTH
    att_cols = 3 * W + HEADS

    w_packed = jnp.concatenate(
        [w_in[:, :, :3 * W], w_in[:, :, att_cols:], w_in[:, :, 3 * W:att_cols],
         jnp.zeros((depth, d, LANES - HEADS), w_in.dtype)], axis=2).astype(BF16)
    w_branch_b = w_branch.astype(BF16)
    w_out_b = w_out.astype(BF16)
    w_mlp1_b = w_mlp1.astype(BF16)
    w_mlp2_b = w_mlp2.astype(BF16)
    pool_w_b = pool_w.astype(BF16)
    b_gate3 = b_gate.reshape(depth, 1, 4 * d)

    mod_all = _ada(c, w_ada, b_ada).reshape(depth, batch, 6, d)
    x2 = x.reshape(batch * seq, d)
    for l in range(depth):
        mod = mod_all[l]
        b_f_row = jnp.pad(b_f[l], (0, LANES - HEADS)).reshape(1, LANES)

        z, zf, h = _inproj(x2, mod, norm_gain[l, 0:1], w_packed, l, seq)
        y_att = _attention(z, zf, b_f_row, jnp.tile(q_gain[l], HEADS_PER_STEP).reshape(1, LANES),
                           jnp.tile(k_gain[l], HEADS_PER_STEP).reshape(1, LANES), batch, seq)
        y_rest = _conv_mixers(z, conf_dw[l], conf_db[l].reshape(1, W), conf_ln_g[l].reshape(1, W),
                              conf_ln_b[l].reshape(1, W), pool_w_b, l,
                              pool_scale[l].reshape(1, W), sconv_w[l], batch, seq)
        x2 = _merge(x2, mod, h, y_att, y_rest, w_packed, b_gate3, w_branch_b, w_out_b, l, seq)
        x2 = _mlp(x2, mod, norm_gain[l, 1:2], w_mlp1_b, w_mlp2_b, l, seq)
    return x2.reshape(batch, seq, d)
```

```python
import functools
import math

import jax
import jax.numpy as jnp
from jax import lax
from jax.experimental import pallas as pl
from jax.experimental.pallas import tpu as pltpu

F32 = jnp.float32
BF16 = jnp.bfloat16

D_MODEL = 2048
WIDTH = D_MODEL // 4
HEADS = 8
HEAD_DIM = WIDTH // HEADS
CONF_K = 31
POOL_WINDOWS = (2, 4, 8, 16)
POOL_GROUP = WIDTH // len(POOL_WINDOWS)
SHORT_K = 3
D_FF = 4 * D_MODEL
RMS_EPS = 1e-6
LN_EPS = 1e-5

LANES = 128
SUBLANES = 8
VMEM_LIMIT = 56 * 1024 * 1024

REPACK_ROWS = 256
ADA_TK = 256
ADA_SPLIT = 4
INPROJ_TM = 1024
INPROJ_TN = 1536
ATT_BQ = 256
HEADS_PER_STEP = LANES // HEAD_DIM
ATT_AUG = 3
LOG2E = math.log2(math.e)
CONV_TT = 512
CONV_HALO = 32
CONV_ROWS = 32
POOL_HIST = 16
MERGE_TM = 1024
MERGE_TN = 256
MERGE_TO = 512
MLP_TM = 1024
MLP_TF = 512

MIX_COLS = 9 * WIDTH
GATE_COL0 = MIX_COLS
FORGET_COL0 = MIX_COLS + 4 * D_MODEL
PACKED_COLS = FORGET_COL0 + LANES


def _params(*sem):
    return pltpu.CompilerParams(dimension_semantics=sem, vmem_limit_bytes=VMEM_LIMIT)


def _sigmoid(x):
    return 1.0 / (1.0 + jnp.exp(-x))


def _modulated_rmsnorm(x, gain, shift, scale):
    y = x * lax.rsqrt(jnp.mean(x * x, axis=-1, keepdims=True) + RMS_EPS) * gain
    return y * (1.0 + scale) + shift


def _repack_kernel(w_ref, o_ref):
    att = 3 * WIDTH
    body = PACKED_COLS - LANES
    o_ref[:, 0:att] = w_ref[:, 0:att].astype(BF16)
    o_ref[:, att:body] = w_ref[:, att + HEADS:body + HEADS].astype(BF16)
    tail = jnp.concatenate([w_ref[:, att:att + HEADS],
                            jnp.zeros((w_ref.shape[0], LANES - HEADS), F32)], axis=1)
    o_ref[:, body:] = tail.astype(BF16)


def _repack_w_in(w_in):
    depth, d, cols = w_in.shape
    return pl.pallas_call(
        _repack_kernel,
        grid=(depth, d // REPACK_ROWS),
        in_specs=[pl.BlockSpec((None, REPACK_ROWS, cols), lambda l, r: (l, r, 0))],
        out_specs=pl.BlockSpec((None, REPACK_ROWS, PACKED_COLS), lambda l, r: (l, r, 0)),
        out_shape=jax.ShapeDtypeStruct((depth, d, PACKED_COLS), BF16),
        compiler_params=_params("arbitrary", "arbitrary"),
        name="repack",
    )(w_in)


def _ada_kernel(c_ref, *refs):
    w_refs, b_ref, o_ref = refs[:ADA_SPLIT], refs[ADA_SPLIT], refs[ADA_SPLIT + 1]

    @pl.when(pl.program_id(1) == 0)
    def _():
        o_ref[...] = jnp.broadcast_to(b_ref[...], o_ref.shape)

    cb = c_ref[...].astype(BF16)
    part = o_ref.shape[1] // ADA_SPLIT
    for s, w_ref in enumerate(w_refs):
        o_ref[:, s * part:(s + 1) * part] += jnp.dot(cb, w_ref[...].astype(BF16),
                                                     preferred_element_type=F32)


def _ada(c, w_ada, b_ada):
    depth, d, n = w_ada.shape
    b = c.shape[0]
    part = n // ADA_SPLIT
    w_spec = lambda s: pl.BlockSpec((None, ADA_TK, part), lambda l, k, s=s: (l, k, s))
    return pl.pallas_call(
        _ada_kernel,
        grid=(depth, d // ADA_TK),
        in_specs=[pl.BlockSpec((b, ADA_TK), lambda l, k: (0, k))]
        + [w_spec(s) for s in range(ADA_SPLIT)]
        + [pl.BlockSpec((None, 1, n), lambda l, k: (l, 0, 0))],
        out_specs=pl.BlockSpec((None, b, n), lambda l, k: (l, 0, 0)),
        out_shape=jax.ShapeDtypeStruct((depth, b, n), F32),
        compiler_params=_params("arbitrary", "arbitrary"),
        name="ada",
    )(c, *([w_ada] * ADA_SPLIT), b_ada.reshape(depth, 1, n))


def _inproj_kernel(x_ref, mod_ref, gain_ref, w_ref, wf_ref, z_ref, zf_ref, h_ref):
    @pl.when(pl.program_id(1) == 0)
    def _():
        h = _modulated_rmsnorm(x_ref[...], gain_ref[...], mod_ref[0:1, :], mod_ref[1:2, :])
        hb = h.astype(BF16)
        h_ref[...] = hb
        zf_ref[...] = jnp.dot(hb, wf_ref[...], preferred_element_type=F32)

    z_ref[...] = jnp.dot(h_ref[...], w_ref[...], preferred_element_type=F32).astype(z_ref.dtype)


def _inproj(x2, mod, gain, w_packed, layer, seq):
    n, d = x2.shape
    tm, tn = INPROJ_TM, INPROJ_TN
    per_batch = seq // tm
    return pl.pallas_call(
        _inproj_kernel,
        grid=(n // tm, MIX_COLS // tn),
        in_specs=[
            pl.BlockSpec((tm, d), lambda i, j: (i, 0)),
            pl.BlockSpec((None, 6, d), lambda i, j: (i // per_batch, 0, 0)),
            pl.BlockSpec((1, d), lambda i, j: (0, 0)),
            pl.BlockSpec((None, d, tn), lambda i, j: (layer, 0, j)),
            pl.BlockSpec((None, d, LANES), lambda i, j: (layer, 0, FORGET_COL0 // LANES)),
        ],
        out_specs=[
            pl.BlockSpec((tm, tn), lambda i, j: (i, j)),
            pl.BlockSpec((tm, LANES), lambda i, j: (i, 0)),
            pl.BlockSpec((tm, d), lambda i, j: (i, 0)),
        ],
        out_shape=[
            jax.ShapeDtypeStruct((n, MIX_COLS), BF16),
            jax.ShapeDtypeStruct((n, LANES), F32),
            jax.ShapeDtypeStruct((n, d), BF16),
        ],
        compiler_params=_params("arbitrary", "arbitrary"),
        name="inproj",
    )(x2, mod, gain, w_packed, w_packed)


def _attn_kernel(q_ref, k_ref, v_ref, zf_ref, bf_ref, qg_ref, kg_ref, o_ref,
                 cum_ref, qa_ref, ka_ref, s_ref, p_ref):
    seq = q_ref.shape[0]
    g = pl.program_id(1)

    @pl.when(g == 0)
    def _():
        logit = zf_ref[...] + bf_ref[...]
        log_f = jnp.minimum(logit, 0.0) - jnp.log1p(jnp.exp(-jnp.abs(logit)))
        acc = log_f.T[0:HEADS, :]
        lane = lax.broadcasted_iota(jnp.int32, acc.shape, 1)
        shift = 1
        while shift < seq:
            acc = acc + jnp.where(lane >= shift, pltpu.roll(acc, shift, axis=1), 0.0)
            shift *= 2
        pad = jnp.zeros((LANES - HEADS, seq), F32)
        rest = jnp.concatenate([acc * LOG2E, pad], axis=0).T
        for j in range(ATT_AUG):
            piece = rest.astype(BF16)
            cum_ref[:, j * LANES:(j + 1) * LANES] = piece
            rest = rest - piece.astype(F32)

    scale = 1.0 / math.sqrt(HEAD_DIM)
    lane = lax.broadcasted_iota(jnp.int32, (seq, LANES), 1)
    in_h0 = lane < HEAD_DIM

    def normalise(ref, gain_ref, mult):
        x = ref[...].astype(F32)
        sq = x * x
        ms0 = jnp.sum(jnp.where(in_h0, sq, 0.0), axis=1, keepdims=True)
        ms1 = jnp.sum(jnp.where(in_h0, 0.0, sq), axis=1, keepdims=True)
        inv = lax.rsqrt(jnp.where(in_h0, ms0, ms1) * (1.0 / HEAD_DIM) + RMS_EPS)
        return x * inv * (gain_ref[...] * mult)

    qn = normalise(q_ref, qg_ref, scale * LOG2E)
    kn = normalise(k_ref, kg_ref, 1.0)

    e_row = lax.broadcasted_iota(jnp.int32, (LANES, LANES), 0)
    e_col = lax.broadcasted_iota(jnp.int32, (LANES, LANES), 1)
    for hh in range(HEADS_PER_STEP):
        head = g * HEADS_PER_STEP + hh
        base = HEAD_DIM * (1 - hh)
        pick = e_row == head
        sel = []
        for j in range(ATT_AUG):
            plus = jnp.where(pick & (e_col == base + j), 1.0, 0.0)
            minus = jnp.where(pick & (e_col == base + ATT_AUG + j), 1.0, 0.0)
            sel.append(plus - minus)
        sel = jnp.concatenate(sel, axis=0).astype(BF16)
        aug = jnp.dot(cum_ref[...], sel, preferred_element_type=F32)
        in_head = in_h0 if hh == 0 else jnp.logical_not(in_h0)
        first = (lane >= base) & (lane < base + ATT_AUG)
        second = (lane >= base + ATT_AUG) & (lane < base + 2 * ATT_AUG)
        qa = jnp.where(in_head, qn, jnp.where(first, aug, jnp.where(second, 1.0, 0.0)))
        ka = jnp.where(in_head, kn, jnp.where(first, 1.0, jnp.where(second, aug, 0.0)))
        qa_ref[hh] = qa.astype(BF16)
        ka_ref[hh] = ka.astype(BF16)

    bq = ATT_BQ
    row = lax.broadcasted_iota(jnp.int32, (bq, bq), 0)
    col = lax.broadcasted_iota(jnp.int32, (bq, bq), 1)
    causal = row >= col
    out_h0 = lax.broadcasted_iota(jnp.int32, (bq, LANES), 1) < HEAD_DIM
    nt = (((1,), (1,)), ((), ()))
    def scores(i, hh, slot):
        qa = qa_ref[hh, i * bq:(i + 1) * bq, :]
        m_part = None
        for c in range(i + 1):
            k0 = c * bq
            s = lax.dot_general(qa, ka_ref[hh, k0:k0 + bq, :], nt, preferred_element_type=F32)
            if c == i:
                s = jnp.where(causal, s, -jnp.inf)
            s_ref[slot, :, k0:k0 + bq] = s
            for h0 in range(0, bq, LANES):
                part = s[:, h0:h0 + LANES]
                m_part = part if m_part is None else jnp.maximum(m_part, part)
        return jnp.max(m_part, axis=1, keepdims=True)

    def weighted_values(i, slot, m):
        l_part = None
        for c in range(i + 1):
            k0 = c * bq
            p = jnp.exp2(s_ref[slot, :, k0:k0 + bq] - m)
            for h0 in range(0, bq, LANES):
                part = p[:, h0:h0 + LANES]
                l_part = part if l_part is None else l_part + part
            p_ref[slot, :, k0:k0 + bq] = p.astype(BF16)
        acc = jnp.dot(p_ref[slot, :, 0:(i + 1) * bq], v_ref[0:(i + 1) * bq, :],
                      preferred_element_type=F32)
        return acc / jnp.sum(l_part, axis=1, keepdims=True)

    units = [(i, hh) for i in range(seq // bq) for hh in range(HEADS_PER_STEP)]
    m_next = scores(*units[0], 0)
    outs = []
    for u, (i, hh) in enumerate(units):
        m_cur = m_next
        if u + 1 < len(units):
            m_next = scores(*units[u + 1], (u + 1) % 2)
        outs.append(weighted_values(i, u % 2, m_cur))
        if hh == HEADS_PER_STEP - 1:
            o_ref[i * bq:(i + 1) * bq, :] = jnp.where(out_h0, outs[0], outs[1]).astype(o_ref.dtype)
            outs = []


def _attention(z, zf, b_f_row, q_gain, k_gain, batch, seq):
    n = z.shape[0]
    groups = HEADS // HEADS_PER_STEP
    blk = lambda off: pl.BlockSpec((seq, LANES), lambda b, g, off=off: (b, off + g))
    return pl.pallas_call(
        _attn_kernel,
        grid=(batch, groups),
        in_specs=[
            blk(0), blk(groups), blk(2 * groups),
            pl.BlockSpec((seq, LANES), lambda b, g: (b, 0)),
            pl.BlockSpec((1, LANES), lambda b, g: (0, 0)),
            pl.BlockSpec((1, LANES), lambda b, g: (0, 0)),
            pl.BlockSpec((1, LANES), lambda b, g: (0, 0)),
        ],
        out_specs=pl.BlockSpec((seq, LANES), lambda b, g: (b, g)),
        out_shape=jax.ShapeDtypeStruct((n, WIDTH), BF16),
        scratch_shapes=[
            pltpu.VMEM((seq, ATT_AUG * LANES), BF16),
            pltpu.VMEM((HEADS_PER_STEP, seq, LANES), BF16),
            pltpu.VMEM((HEADS_PER_STEP, seq, LANES), BF16),
            pltpu.VMEM((2, ATT_BQ, seq), F32),
            pltpu.VMEM((2, ATT_BQ, seq), BF16),
        ],
        compiler_params=_params("arbitrary", "arbitrary"),
        name="attn",
    )(z, z, z, zf, b_f_row, q_gain, k_gain)


def _conv_kernel(a_ref, g_ref, p_ref, x_ref, b_ref, c_ref,
                 ah_ref, gh_ref, ph_ref, xh_ref, ch_ref,
                 dw_ref, db_ref, lng_ref, lnb_ref, pw_ref, ps_ref, sw_ref,
                 o_ref, ubuf, pbuf, vbuf):
    tt = a_ref.shape[0]
    t = pl.program_id(1)
    has_past = t > 0
    H = CONV_HALO

    u_h = ah_ref[...].astype(F32) * _sigmoid(gh_ref[...].astype(F32))
    ubuf[0:H, :] = jnp.where(has_past, u_h, 0.0)
    ubuf[H:, :] = a_ref[...].astype(F32) * _sigmoid(g_ref[...].astype(F32))
    pbuf[0:H, :] = jnp.where(has_past, ph_ref[...].astype(F32), 0.0)
    pbuf[H:, :] = p_ref[...].astype(F32)
    v_h = ch_ref[...].astype(F32) * xh_ref[...].astype(F32)
    vbuf[0:H, :] = jnp.where(has_past, v_h, 0.0)
    vbuf[H:, :] = c_ref[...].astype(F32) * x_ref[...].astype(F32)

    R = CONV_ROWS

    def chunk(ci, carry):
        r0 = pl.multiple_of(ci * R, R)

        acc = jnp.broadcast_to(db_ref[...], (R, WIDTH))
        win = ubuf[pl.ds(r0, R + H), :]
        for r in range(SUBLANES):
            rolled = win if r == 0 else pltpu.roll(win, r, axis=0)
            for a in range((CONF_K - 1 - r) // SUBLANES + 1):
                k = CONF_K - 1 - (SUBLANES * a + r)
                lo = H - SUBLANES * a
                acc = acc + dw_ref[k:k + 1, :] * rolled[lo:lo + R, :]
        mu = jnp.mean(acc, axis=-1, keepdims=True)
        xc = acc - mu
        y = xc * lax.rsqrt(jnp.mean(xc * xc, axis=-1, keepdims=True) + LN_EPS)
        y = y * lng_ref[...] + lnb_ref[...]
        o_ref[pl.ds(r0, R), 0:WIDTH] = (y * _sigmoid(y)).astype(o_ref.dtype)

        pos = t * tt + r0 + lax.broadcasted_iota(jnp.int32, (R, 1), 0)
        pooled = []
        for gi, w in enumerate(POOL_WINDOWS):
            lo = gi * POOL_GROUP
            tot = pbuf[pl.ds(r0 + (H - POOL_HIST), R + POOL_HIST), lo:lo + POOL_GROUP]
            tok = tot[POOL_HIST:, :]
            span = 1
            while span < w:
                tot = tot + pltpu.roll(tot, span, axis=0)
                span *= 2
            cnt = jnp.minimum(pos + 1, w).astype(F32)
            dlt = (tot[POOL_HIST:, :] / cnt - tok).astype(BF16)
            pooled.append(jnp.dot(dlt, pw_ref[gi], preferred_element_type=F32))
        yp = jnp.concatenate(pooled, axis=1) * ps_ref[...]
        o_ref[pl.ds(r0, R), WIDTH:2 * WIDTH] = yp.astype(o_ref.dtype)

        vwin = vbuf[pl.ds(r0 + (H - SUBLANES), R + SUBLANES), :]
        sc = sw_ref[SHORT_K - 1:SHORT_K, :] * vwin[SUBLANES:, :]
        for dly in range(1, SHORT_K):
            k = SHORT_K - 1 - dly
            sc = sc + sw_ref[k:k + 1, :] * pltpu.roll(vwin, dly, axis=0)[SUBLANES:, :]
        ys = b_ref[pl.ds(r0, R), :].astype(F32) * sc
        o_ref[pl.ds(r0, R), 2 * WIDTH:3 * WIDTH] = ys.astype(o_ref.dtype)
        return carry

    lax.fori_loop(0, tt // R, chunk, 0, unroll=2)


def _conv_mixers(z, conf_dw, conf_db, ln_g, ln_b, pool_w, layer, pool_scale, sconv_w, batch, seq):
    n = z.shape[0]
    tt, H = CONV_TT, CONV_HALO
    tiles = seq // tt
    main = lambda cb: pl.BlockSpec((tt, WIDTH), lambda b, t, cb=cb: (b * tiles + t, cb))
    halo = lambda cb: pl.BlockSpec(
        (H, WIDTH),
        lambda b, t, cb=cb: (jnp.maximum((b * tiles + t) * (tt // H) - 1, 0), cb))
    full = lambda shape: pl.BlockSpec(shape, lambda b, t: (0,) * len(shape))
    return pl.pallas_call(
        _conv_kernel,
        grid=(batch, tiles),
        in_specs=[
            main(3), main(4), main(5), main(6), main(7), main(8),
            halo(3), halo(4), halo(5), halo(6), halo(8),
            full((CONF_K, WIDTH)), full((1, WIDTH)), full((1, WIDTH)), full((1, WIDTH)),
            pl.BlockSpec((None, len(POOL_WINDOWS), POOL_GROUP, POOL_GROUP),
                         lambda b, t: (layer, 0, 0, 0)),
            full((1, WIDTH)),
            full((SHORT_K, WIDTH)),
        ],
        out_specs=pl.BlockSpec((tt, 3 * WIDTH), lambda b, t: (b * tiles + t, 0)),
        out_shape=jax.ShapeDtypeStruct((n, 3 * WIDTH), BF16),
        scratch_shapes=[pltpu.VMEM((H + tt, WIDTH), F32)] * 3,
        compiler_params=_params("arbitrary", "arbitrary"),
        name="conv",
    )(z, z, z, z, z, z, z, z, z, z, z,
      conf_dw, conf_db, ln_g, ln_b, pool_w, pool_scale, sconv_w)


def _merge_kernel(x_ref, mod_ref, h_ref, ya_ref, yr_ref,
                  wg0_ref, wg1_ref, wg2_ref, wg3_ref, bg0_ref, bg1_ref, bg2_ref, bg3_ref,
                  wbr_ref, wout_ref, o_ref, m_ref):
    j = pl.program_id(1)
    na = m_ref.shape[0]

    @pl.when(j < na)
    def _():
        h = h_ref[...]
        wgs = (wg0_ref, wg1_ref, wg2_ref, wg3_ref)
        bgs = (bg0_ref, bg1_ref, bg2_ref, bg3_ref)
        merged = None
        for b in range(4):
            y = ya_ref[...] if b == 0 else yr_ref[:, (b - 1) * WIDTH:b * WIDTH]
            gate = _sigmoid(jnp.dot(h, wgs[b][...], preferred_element_type=F32) + bgs[b][...])
            term = gate * jnp.dot(y, wbr_ref[b], preferred_element_type=F32)
            merged = term if merged is None else merged + term
        m_ref[j] = merged.astype(BF16)

    @pl.when(j >= na)
    def _():
        merged = jnp.concatenate([m_ref[k] for k in range(na)], axis=1)
        proj = jnp.dot(merged, wout_ref[j - na], preferred_element_type=F32)
        o_ref[...] = x_ref[...] + mod_ref[2:3, :] * proj


def _merge(x2, mod, h, y_att, y_rest, w_packed, b_gate, w_branch, w_out, layer, seq):
    n, d = x2.shape
    tm, tn, to = MERGE_TM, MERGE_TN, MERGE_TO
    per_batch = seq // tm
    na, nb = d // tn, d // to
    ja = lambda j: jnp.minimum(j, na - 1)
    jb = lambda j: jnp.maximum(j - na, 0)
    gate0 = GATE_COL0 // tn
    wg = lambda b: pl.BlockSpec((None, d, tn), lambda i, j, b=b: (layer, 0, gate0 + b * na + ja(j)))
    bg = lambda b: pl.BlockSpec((None, 1, tn), lambda i, j, b=b: (layer, 0, b * na + ja(j)))
    return pl.pallas_call(
        _merge_kernel,
        grid=(n // tm, na + nb),
        in_specs=[
            pl.BlockSpec((tm, to), lambda i, j: (i, jb(j))),
            pl.BlockSpec((None, 6, to), lambda i, j: (i // per_batch, 0, jb(j))),
            pl.BlockSpec((tm, d), lambda i, j: (i, 0)),
            pl.BlockSpec((tm, WIDTH), lambda i, j: (i, 0)),
            pl.BlockSpec((tm, 3 * WIDTH), lambda i, j: (i, 0)),
            wg(0), wg(1), wg(2), wg(3), bg(0), bg(1), bg(2), bg(3),
            pl.BlockSpec((None, 4, WIDTH, tn), lambda i, j: (layer, 0, 0, ja(j))),
            pl.BlockSpec((None, nb, d, to), lambda i, j: (layer, 0, 0, 0),
                         pipeline_mode=pl.Buffered(1)),
        ],
        out_specs=pl.BlockSpec((tm, to), lambda i, j: (i, jb(j))),
        out_shape=jax.ShapeDtypeStruct((n, d), F32),
        scratch_shapes=[pltpu.VMEM((na, tm, tn), BF16)],
        compiler_params=_params("arbitrary", "arbitrary"),
        name="merge",
    )(x2, mod, h, y_att, y_rest, w_packed, w_packed, w_packed, w_packed,
      b_gate, b_gate, b_gate, b_gate, w_branch, w_out)


def _mlp_kernel(x_ref, mod_ref, gain_ref, w1_ref, w2_ref, o_ref, h_ref):
    j = pl.program_id(1)

    @pl.when(j == 0)
    def _():
        h = _modulated_rmsnorm(x_ref[...], gain_ref[...], mod_ref[3:4, :], mod_ref[4:5, :])
        h_ref[...] = h.astype(BF16)
        o_ref[...] = jnp.zeros_like(o_ref)

    hid = jnp.maximum(jnp.dot(h_ref[...], w1_ref[...], preferred_element_type=F32), 0.0)
    o_ref[...] += jnp.dot((hid * hid).astype(BF16), w2_ref[...], preferred_element_type=F32)

    @pl.when(j == pl.num_programs(1) - 1)
    def _():
        o_ref[...] = x_ref[...] + mod_ref[5:6, :] * o_ref[...]


def _mlp(x2, mod, gain, w1, w2, layer, seq):
    n, d = x2.shape
    tm, tf = MLP_TM, MLP_TF
    per_batch = seq // tm
    return pl.pallas_call(
        _mlp_kernel,
        grid=(n // tm, D_FF // tf),
        in_specs=[
            pl.BlockSpec((tm, d), lambda i, j: (i, 0)),
            pl.BlockSpec((None, 6, d), lambda i, j: (i // per_batch, 0, 0)),
            pl.BlockSpec((1, d), lambda i, j: (0, 0)),
            pl.BlockSpec((None, d, tf), lambda i, j: (layer, 0, j)),
            pl.BlockSpec((None, tf, d), lambda i, j: (layer, j, 0)),
        ],
        out_specs=pl.BlockSpec((tm, d), lambda i, j: (i, 0)),
        out_shape=jax.ShapeDtypeStruct((n, d), F32),
        scratch_shapes=[pltpu.VMEM((tm, d), BF16)],
        compiler_params=_params("arbitrary", "arbitrary"),
        name="mlp",
    )(x2, mod, gain, w1, w2)


def kernel(x, c, w_ada, b_ada, norm_gain, w_in, b_f, b_gate, q_gain, k_gain, conf_dw, conf_db,
           conf_ln_g, conf_ln_b, pool_w, pool_scale, sconv_w, w_branch, w_out, w_mlp1, w_mlp2):
    batch, seq, d = x.shape
    depth = w_ada.shape[0]
    W = WIDTH
    att_cols = 3 * W + HEADS

    w_packed = _repack_w_in(w_in)
    w_branch_b = w_branch.astype(BF16)
    w_out_b = (w_out.astype(BF16).reshape(depth, d, d // MERGE_TO, MERGE_TO)
               .transpose(0, 2, 1, 3))
    w_mlp1_b = w_mlp1.astype(BF16)
    w_mlp2_b = w_mlp2.astype(BF16)
    pool_w_b = pool_w.astype(BF16)
    b_gate3 = b_gate.reshape(depth, 1, 4 * d)

    mod_all = _ada(c, w_ada, b_ada).reshape(depth, batch, 6, d)
    x2 = x.reshape(batch * seq, d)
    for l in range(depth):
        mod = mod_all[l]
        b_f_row = jnp.pad(b_f[l], (0, LANES - HEADS)).reshape(1, LANES)

        z, zf, h = _inproj(x2, mod, norm_gain[l, 0:1], w_packed, l, seq)
        y_att = _attention(z, zf, b_f_row, jnp.tile(q_gain[l], HEADS_PER_STEP).reshape(1, LANES),
                           jnp.tile(k_gain[l], HEADS_PER_STEP).reshape(1, LANES), batch, seq)
        y_rest = _conv_mixers(z, conf_dw[l], conf_db[l].reshape(1, W), conf_ln_g[l].reshape(1, W),
                              conf_ln_b[l].reshape(1, W), pool_w_b, l,
                              pool_scale[l].reshape(1, W), sconv_w[l], batch, seq)
        x2 = _merge(x2, mod, h, y_att, y_rest, w_packed, b_gate3, w_branch_b, w_out_b, l, seq)
        x2 = _mlp(x2, mod, norm_gain[l, 1:2], w_mlp1_b, w_mlp2_b, l, seq)
    return x2.reshape(batch, seq, d)
```

```python
import functools
import math

import jax
import jax.numpy as jnp
from jax import lax
from jax.experimental import pallas as pl
from jax.experimental.pallas import tpu as pltpu

F32 = jnp.float32
BF16 = jnp.bfloat16

D_MODEL = 2048
WIDTH = D_MODEL // 4
HEADS = 8
HEAD_DIM = WIDTH // HEADS
CONF_K = 31
POOL_WINDOWS = (2, 4, 8, 16)
POOL_GROUP = WIDTH // len(POOL_WINDOWS)
SHORT_K = 3
D_FF = 4 * D_MODEL
RMS_EPS = 1e-6
LN_EPS = 1e-5

LANES = 128
SUBLANES = 8
VMEM_LIMIT = 56 * 1024 * 1024

ADA_TK = 256
ADA_SPLIT = 4
INPROJ_TM = 1024
INPROJ_TN = 1536
ATT_BQ = 256
HEADS_PER_STEP = LANES // HEAD_DIM
ATT_AUG = 3
LOG2E = math.log2(math.e)
CONV_TT = 512
CONV_HALO = 32
CONV_ROWS = 32
POOL_HIST = 16
MERGE_TM = 1024
MERGE_TN = 256
MERGE_TO = 512
MLP_TM = 1024
MLP_TF = 512

MIX_COLS = 9 * WIDTH
GATE_COL0 = MIX_COLS
PACKED_COLS = MIX_COLS + 4 * D_MODEL
ATT_COLS = 3 * WIDTH
REPACK_ROWS = 512


def _params(*sem):
    return pltpu.CompilerParams(dimension_semantics=sem, vmem_limit_bytes=VMEM_LIMIT)


def _sigmoid(x):
    return 1.0 / (1.0 + jnp.exp(-x))


def _modulated_rmsnorm(x, gain, shift, scale):
    y = x * lax.rsqrt(jnp.mean(x * x, axis=-1, keepdims=True) + RMS_EPS) * gain
    return y * (1.0 + scale) + shift


_NT = (((1,), (1,)), ((), ()))


def _repack_kernel(w_ref, tail_ref, o_ref):
    r = pl.program_id(1)
    keep = ATT_COLS // REPACK_ROWS

    @pl.when(r < keep)
    def _():
        o_ref[...] = w_ref[...].astype(BF16)

    @pl.when(r >= keep)
    def _():
        moved = jnp.concatenate([w_ref[HEADS:, :], tail_ref[...]], axis=0)
        o_ref[...] = moved.astype(BF16)


def _repack_w_in(w_t):
    depth, cols, d = w_t.shape
    tr = REPACK_ROWS
    return pl.pallas_call(
        _repack_kernel,
        grid=(depth, PACKED_COLS // tr),
        in_specs=[
            pl.BlockSpec((None, tr, d), lambda l, r: (l, r, 0)),
            pl.BlockSpec((None, HEADS, d), lambda l, r: (l, (r + 1) * (tr // HEADS), 0)),
        ],
        out_specs=pl.BlockSpec((None, tr, d), lambda l, r: (l, r, 0)),
        out_shape=jax.ShapeDtypeStruct((depth, PACKED_COLS, d), BF16),
        compiler_params=_params("arbitrary", "arbitrary"),
        name="repack",
    )(w_t, w_t)


def _ada_kernel(c_ref, *refs):
    w_refs, b_ref, o_ref = refs[:ADA_SPLIT], refs[ADA_SPLIT], refs[ADA_SPLIT + 1]

    @pl.when(pl.program_id(1) == 0)
    def _():
        o_ref[...] = jnp.broadcast_to(b_ref[...], o_ref.shape)

    cb = c_ref[...].astype(BF16)
    part = o_ref.shape[1] // ADA_SPLIT
    for s, w_ref in enumerate(w_refs):
        o_ref[:, s * part:(s + 1) * part] += jnp.dot(cb, w_ref[...].astype(BF16),
                                                     preferred_element_type=F32)


def _ada(c, w_ada, b_ada):
    depth, d, n = w_ada.shape
    b = c.shape[0]
    part = n // ADA_SPLIT
    w_spec = lambda s: pl.BlockSpec((None, ADA_TK, part), lambda l, k, s=s: (l, k, s))
    return pl.pallas_call(
        _ada_kernel,
        grid=(depth, d // ADA_TK),
        in_specs=[pl.BlockSpec((b, ADA_TK), lambda l, k: (0, k))]
        + [w_spec(s) for s in range(ADA_SPLIT)]
        + [pl.BlockSpec((None, 1, n), lambda l, k: (l, 0, 0))],
        out_specs=pl.BlockSpec((None, b, n), lambda l, k: (l, 0, 0)),
        out_shape=jax.ShapeDtypeStruct((depth, b, n), F32),
        compiler_params=_params("arbitrary", "arbitrary"),
        name="ada",
    )(c, *([w_ada] * ADA_SPLIT), b_ada.reshape(depth, 1, n))


def _inproj_kernel(x_ref, mod_ref, gain_ref, w_ref, wf_ref, z_ref, zf_ref, h_ref):
    @pl.when(pl.program_id(1) == 0)
    def _():
        h = _modulated_rmsnorm(x_ref[...], gain_ref[...], mod_ref[0:1, :], mod_ref[1:2, :])
        hb = h.astype(BF16)
        h_ref[...] = hb
        zf_ref[...] = lax.dot_general(hb, wf_ref[...], _NT, preferred_element_type=F32)

    z_ref[...] = lax.dot_general(h_ref[...], w_ref[...], _NT,
                                 preferred_element_type=F32).astype(z_ref.dtype)


def _inproj(x2, mod, gain, w_packed, w_forget, layer, seq):
    n, d = x2.shape
    tm, tn = INPROJ_TM, INPROJ_TN
    per_batch = seq // tm
    return pl.pallas_call(
        _inproj_kernel,
        grid=(n // tm, MIX_COLS // tn),
        in_specs=[
            pl.BlockSpec((tm, d), lambda i, j: (i, 0)),
            pl.BlockSpec((None, 6, d), lambda i, j: (i // per_batch, 0, 0)),
            pl.BlockSpec((1, d), lambda i, j: (0, 0)),
            pl.BlockSpec((None, tn, d), lambda i, j: (layer, j, 0)),
            pl.BlockSpec((None, LANES, d), lambda i, j: (layer, 0, 0)),
        ],
        out_specs=[
            pl.BlockSpec((tm, tn), lambda i, j: (i, j)),
            pl.BlockSpec((tm, LANES), lambda i, j: (i, 0)),
            pl.BlockSpec((tm, d), lambda i, j: (i, 0)),
        ],
        out_shape=[
            jax.ShapeDtypeStruct((n, MIX_COLS), BF16),
            jax.ShapeDtypeStruct((n, LANES), F32),
            jax.ShapeDtypeStruct((n, d), BF16),
        ],
        compiler_params=_params("arbitrary", "arbitrary"),
        name="inproj",
    )(x2, mod, gain, w_packed, w_forget)


def _attn_kernel(q_ref, k_ref, v_ref, zf_ref, bf_ref, qg_ref, kg_ref, o_ref,
                 cum_ref, qa_ref, ka_ref, s_ref, p_ref):
    seq = q_ref.shape[0]
    g = pl.program_id(1)

    @pl.when(g == 0)
    def _():
        logit = zf_ref[...] + bf_ref[...]
        log_f = jnp.minimum(logit, 0.0) - jnp.log1p(jnp.exp(-jnp.abs(logit)))
        acc = log_f.T[0:HEADS, :]
        lane = lax.broadcasted_iota(jnp.int32, acc.shape, 1)
        shift = 1
        while shift < seq:
            acc = acc + jnp.where(lane >= shift, pltpu.roll(acc, shift, axis=1), 0.0)
            shift *= 2
        pad = jnp.zeros((LANES - HEADS, seq), F32)
        rest = jnp.concatenate([acc * LOG2E, pad], axis=0).T
        for j in range(ATT_AUG):
            piece = rest.astype(BF16)
            cum_ref[:, j * LANES:(j + 1) * LANES] = piece
            rest = rest - piece.astype(F32)

    scale = 1.0 / math.sqrt(HEAD_DIM)
    lane = lax.broadcasted_iota(jnp.int32, (seq, LANES), 1)
    in_h0 = lane < HEAD_DIM

    def normalise(ref, gain_ref, mult):
        x = ref[...].astype(F32)
        sq = x * x
        ms0 = jnp.sum(jnp.where(in_h0, sq, 0.0), axis=1, keepdims=True)
        ms1 = jnp.sum(jnp.where(in_h0, 0.0, sq), axis=1, keepdims=True)
        inv = lax.rsqrt(jnp.where(in_h0, ms0, ms1) * (1.0 / HEAD_DIM) + RMS_EPS)
        return x * inv * (gain_ref[...] * mult)

    qn = normalise(q_ref, qg_ref, scale * LOG2E)
    kn = normalise(k_ref, kg_ref, 1.0)

    e_row = lax.broadcasted_iota(jnp.int32, (LANES, LANES), 0)
    e_col = lax.broadcasted_iota(jnp.int32, (LANES, LANES), 1)
    for hh in range(HEADS_PER_STEP):
        head = g * HEADS_PER_STEP + hh
        base = HEAD_DIM * (1 - hh)
        pick = e_row == head
        sel = []
        for j in range(ATT_AUG):
            plus = jnp.where(pick & (e_col == base + j), 1.0, 0.0)
            minus = jnp.where(pick & (e_col == base + ATT_AUG + j), 1.0, 0.0)
            sel.append(plus - minus)
        sel = jnp.concatenate(sel, axis=0).astype(BF16)
        aug = jnp.dot(cum_ref[...], sel, preferred_element_type=F32)
        in_head = in_h0 if hh == 0 else jnp.logical_not(in_h0)
        first = (lane >= base) & (lane < base + ATT_AUG)
        second = (lane >= base + ATT_AUG) & (lane < base + 2 * ATT_AUG)
        qa = jnp.where(in_head, qn, jnp.where(first, aug, jnp.where(second, 1.0, 0.0)))
        ka = jnp.where(in_head, kn, jnp.where(first, 1.0, jnp.where(second, aug, 0.0)))
        qa_ref[hh] = qa.astype(BF16)
        ka_ref[hh] = ka.astype(BF16)

    bq = ATT_BQ
    row = lax.broadcasted_iota(jnp.int32, (bq, bq), 0)
    col = lax.broadcasted_iota(jnp.int32, (bq, bq), 1)
    causal = row >= col
    out_h0 = lax.broadcasted_iota(jnp.int32, (bq, LANES), 1) < HEAD_DIM
    nt = (((1,), (1,)), ((), ()))
    def scores(i, hh, slot):
        qa = qa_ref[hh, i * bq:(i + 1) * bq, :]
        m_part = None
        for c in range(i + 1):
            k0 = c * bq
            s = lax.dot_general(qa, ka_ref[hh, k0:k0 + bq, :], nt, preferred_element_type=F32)
            if c == i:
                s = jnp.where(causal, s, -jnp.inf)
            s_ref[slot, :, k0:k0 + bq] = s
            for h0 in range(0, bq, LANES):
                part = s[:, h0:h0 + LANES]
                m_part = part if m_part is None else jnp.maximum(m_part, part)
        return jnp.max(m_part, axis=1, keepdims=True)

    def weighted_values(i, slot, m):
        l_part = None
        for c in range(i + 1):
            k0 = c * bq
            p = jnp.exp2(s_ref[slot, :, k0:k0 + bq] - m)
            for h0 in range(0, bq, LANES):
                part = p[:, h0:h0 + LANES]
                l_part = part if l_part is None else l_part + part
            p_ref[slot, :, k0:k0 + bq] = p.astype(BF16)
        acc = jnp.dot(p_ref[slot, :, 0:(i + 1) * bq], v_ref[0:(i + 1) * bq, :],
                      preferred_element_type=F32)
        return acc / jnp.sum(l_part, axis=1, keepdims=True)

    units = [(i, hh) for i in range(seq // bq) for hh in range(HEADS_PER_STEP)]
    m_next = scores(*units[0], 0)
    outs = []
    for u, (i, hh) in enumerate(units):
        m_cur = m_next
        if u + 1 < len(units):
            m_next = scores(*units[u + 1], (u + 1) % 2)
        outs.append(weighted_values(i, u % 2, m_cur))
        if hh == HEADS_PER_STEP - 1:
            o_ref[i * bq:(i + 1) * bq, :] = jnp.where(out_h0, outs[0], outs[1]).astype(o_ref.dtype)
            outs = []


def _attention(z, zf, b_f_row, q_gain, k_gain, batch, seq):
    n = z.shape[0]
    groups = HEADS // HEADS_PER_STEP
    blk = lambda off: pl.BlockSpec((seq, LANES), lambda b, g, off=off: (b, off + g))
    return pl.pallas_call(
        _attn_kernel,
        grid=(batch, groups),
        in_specs=[
            blk(0), blk(groups), blk(2 * groups),
            pl.BlockSpec((seq, LANES), lambda b, g: (b, 0)),
            pl.BlockSpec((1, LANES), lambda b, g: (0, 0)),
            pl.BlockSpec((1, LANES), lambda b, g: (0, 0)),
            pl.BlockSpec((1, LANES), lambda b, g: (0, 0)),
        ],
        out_specs=pl.BlockSpec((seq, LANES), lambda b, g: (b, g)),
        out_shape=jax.ShapeDtypeStruct((n, WIDTH), BF16),
        scratch_shapes=[
            pltpu.VMEM((seq, ATT_AUG * LANES), BF16),
            pltpu.VMEM((HEADS_PER_STEP, seq, LANES), BF16),
            pltpu.VMEM((HEADS_PER_STEP, seq, LANES), BF16),
            pltpu.VMEM((2, ATT_BQ, seq), F32),
            pltpu.VMEM((2, ATT_BQ, seq), BF16),
        ],
        compiler_params=_params("arbitrary", "arbitrary"),
        name="attn",
    )(z, z, z, zf, b_f_row, q_gain, k_gain)


def _conv_kernel(a_ref, g_ref, p_ref, x_ref, b_ref, c_ref,
                 ah_ref, gh_ref, ph_ref, xh_ref, ch_ref,
                 dw_ref, db_ref, lng_ref, lnb_ref, pw_ref, ps_ref, sw_ref,
                 o_ref, ubuf, pbuf, vbuf):
    tt = a_ref.shape[0]
    t = pl.program_id(1)
    has_past = t > 0
    H = CONV_HALO

    u_h = ah_ref[...].astype(F32) * _sigmoid(gh_ref[...].astype(F32))
    ubuf[0:H, :] = jnp.where(has_past, u_h, 0.0)
    ubuf[H:, :] = a_ref[...].astype(F32) * _sigmoid(g_ref[...].astype(F32))
    pbuf[0:H, :] = jnp.where(has_past, ph_ref[...].astype(F32), 0.0)
    pbuf[H:, :] = p_ref[...].astype(F32)
    v_h = ch_ref[...].astype(F32) * xh_ref[...].astype(F32)
    vbuf[0:H, :] = jnp.where(has_past, v_h, 0.0)
    vbuf[H:, :] = c_ref[...].astype(F32) * x_ref[...].astype(F32)

    R = CONV_ROWS

    def chunk(ci, carry):
        r0 = pl.multiple_of(ci * R, R)

        acc = jnp.broadcast_to(db_ref[...], (R, WIDTH))
        win = ubuf[pl.ds(r0, R + H), :]
        for r in range(SUBLANES):
            rolled = win if r == 0 else pltpu.roll(win, r, axis=0)
            for a in range((CONF_K - 1 - r) // SUBLANES + 1):
                k = CONF_K - 1 - (SUBLANES * a + r)
                lo = H - SUBLANES * a
                acc = acc + dw_ref[k:k + 1, :] * rolled[lo:lo + R, :]
        mu = jnp.mean(acc, axis=-1, keepdims=True)
        xc = acc - mu
        y = xc * lax.rsqrt(jnp.mean(xc * xc, axis=-1, keepdims=True) + LN_EPS)
        y = y * lng_ref[...] + lnb_ref[...]
        o_ref[pl.ds(r0, R), 0:WIDTH] = (y * _sigmoid(y)).astype(o_ref.dtype)

        pos = t * tt + r0 + lax.broadcasted_iota(jnp.int32, (R, 1), 0)
        pooled = []
        for gi, w in enumerate(POOL_WINDOWS):
            lo = gi * POOL_GROUP
            tot = pbuf[pl.ds(r0 + (H - POOL_HIST), R + POOL_HIST), lo:lo + POOL_GROUP]
            tok = tot[POOL_HIST:, :]
            span = 1
            while span < w:
                tot = tot + pltpu.roll(tot, span, axis=0)
                span *= 2
            cnt = jnp.minimum(pos + 1, w).astype(F32)
            dlt = (tot[POOL_HIST:, :] / cnt - tok).astype(BF16)
            pooled.append(jnp.dot(dlt, pw_ref[gi], preferred_element_type=F32))
        yp = jnp.concatenate(pooled, axis=1) * ps_ref[...]
        o_ref[pl.ds(r0, R), WIDTH:2 * WIDTH] = yp.astype(o_ref.dtype)

        vwin = vbuf[pl.ds(r0 + (H - SUBLANES), R + SUBLANES), :]
        sc = sw_ref[SHORT_K - 1:SHORT_K, :] * vwin[SUBLANES:, :]
        for dly in range(1, SHORT_K):
            k = SHORT_K - 1 - dly
            sc = sc + sw_ref[k:k + 1, :] * pltpu.roll(vwin, dly, axis=0)[SUBLANES:, :]
        ys = b_ref[pl.ds(r0, R), :].astype(F32) * sc
        o_ref[pl.ds(r0, R), 2 * WIDTH:3 * WIDTH] = ys.astype(o_ref.dtype)
        return carry

    lax.fori_loop(0, tt // R, chunk, 0, unroll=2)


def _conv_mixers(z, conf_dw, conf_db, ln_g, ln_b, pool_w, layer, pool_scale, sconv_w, batch, seq):
    n = z.shape[0]
    tt, H = CONV_TT, CONV_HALO
    tiles = seq // tt
    main = lambda cb: pl.BlockSpec((tt, WIDTH), lambda b, t, cb=cb: (b * tiles + t, cb))
    halo = lambda cb: pl.BlockSpec(
        (H, WIDTH),
        lambda b, t, cb=cb: (jnp.maximum((b * tiles + t) * (tt // H) - 1, 0), cb))
    full = lambda shape: pl.BlockSpec(shape, lambda b, t: (0,) * len(shape))
    return pl.pallas_call(
        _conv_kernel,
        grid=(batch, tiles),
        in_specs=[
            main(3), main(4), main(5), main(6), main(7), main(8),
            halo(3), halo(4), halo(5), halo(6), halo(8),
            full((CONF_K, WIDTH)), full((1, WIDTH)), full((1, WIDTH)), full((1, WIDTH)),
            pl.BlockSpec((None, len(POOL_WINDOWS), POOL_GROUP, POOL_GROUP),
                         lambda b, t: (layer, 0, 0, 0)),
            full((1, WIDTH)),
            full((SHORT_K, WIDTH)),
        ],
        out_specs=pl.BlockSpec((tt, 3 * WIDTH), lambda b, t: (b * tiles + t, 0)),
        out_shape=jax.ShapeDtypeStruct((n, 3 * WIDTH), BF16),
        scratch_shapes=[pltpu.VMEM((H + tt, WIDTH), F32)] * 3,
        compiler_params=_params("arbitrary", "arbitrary"),
        name="conv",
    )(z, z, z, z, z, z, z, z, z, z, z,
      conf_dw, conf_db, ln_g, ln_b, pool_w, pool_scale, sconv_w)


def _merge_kernel(x_ref, mod_ref, h_ref, ya_ref, yr_ref,
                  wg0_ref, wg1_ref, wg2_ref, wg3_ref, bg0_ref, bg1_ref, bg2_ref, bg3_ref,
                  wbr_ref, wout_ref, o_ref, m_ref):
    j = pl.program_id(1)
    na = m_ref.shape[0]

    @pl.when(j < na)
    def _():
        h = h_ref[...]
        wgs = (wg0_ref, wg1_ref, wg2_ref, wg3_ref)
        bgs = (bg0_ref, bg1_ref, bg2_ref, bg3_ref)
        merged = None
        for b in range(4):
            y = ya_ref[...] if b == 0 else yr_ref[:, (b - 1) * WIDTH:b * WIDTH]
            logit = lax.dot_general(h, wgs[b][...], _NT, preferred_element_type=F32)
            gate = _sigmoid(logit + bgs[b][...])
            term = gate * jnp.dot(y, wbr_ref[b], preferred_element_type=F32)
            merged = term if merged is None else merged + term
        m_ref[j] = merged.astype(BF16)

    @pl.when(j >= na)
    def _():
        merged = jnp.concatenate([m_ref[k] for k in range(na)], axis=1)
        proj = jnp.dot(merged, wout_ref[j - na], preferred_element_type=F32)
        o_ref[...] = x_ref[...] + mod_ref[2:3, :] * proj


def _merge(x2, mod, h, y_att, y_rest, w_packed, b_gate, w_branch, w_out, layer, seq):
    n, d = x2.shape
    tm, tn, to = MERGE_TM, MERGE_TN, MERGE_TO
    per_batch = seq // tm
    na, nb = d // tn, d // to
    ja = lambda j: jnp.minimum(j, na - 1)
    jb = lambda j: jnp.maximum(j - na, 0)
    gate0 = GATE_COL0 // tn
    wg = lambda b: pl.BlockSpec((None, tn, d), lambda i, j, b=b: (layer, gate0 + b * na + ja(j), 0))
    bg = lambda b: pl.BlockSpec((None, 1, tn), lambda i, j, b=b: (layer, 0, b * na + ja(j)))
    return pl.pallas_call(
        _merge_kernel,
        grid=(n // tm, na + nb),
        in_specs=[
            pl.BlockSpec((tm, to), lambda i, j: (i, jb(j))),
            pl.BlockSpec((None, 6, to), lambda i, j: (i // per_batch, 0, jb(j))),
            pl.BlockSpec((tm, d), lambda i, j: (i, 0)),
            pl.BlockSpec((tm, WIDTH), lambda i, j: (i, 0)),
            pl.BlockSpec((tm, 3 * WIDTH), lambda i, j: (i, 0)),
            wg(0), wg(1), wg(2), wg(3), bg(0), bg(1), bg(2), bg(3),
            pl.BlockSpec((None, 4, WIDTH, tn), lambda i, j: (layer, 0, 0, ja(j))),
            pl.BlockSpec((None, nb, d, to), lambda i, j: (layer, 0, 0, 0),
                         pipeline_mode=pl.Buffered(1)),
        ],
        out_specs=pl.BlockSpec((tm, to), lambda i, j: (i, jb(j))),
        out_shape=jax.ShapeDtypeStruct((n, d), F32),
        scratch_shapes=[pltpu.VMEM((na, tm, tn), BF16)],
        compiler_params=_params("arbitrary", "arbitrary"),
        name="merge",
    )(x2, mod, h, y_att, y_rest, w_packed, w_packed, w_packed, w_packed,
      b_gate, b_gate, b_gate, b_gate, w_branch, w_out)


def _mlp_kernel(x_ref, mod_ref, gain_ref, w1_ref, w2_ref, o_ref, h_ref):
    j = pl.program_id(1)

    @pl.when(j == 0)
    def _():
        h = _modulated_rmsnorm(x_ref[...], gain_ref[...], mod_ref[3:4, :], mod_ref[4:5, :])
        h_ref[...] = h.astype(BF16)
        o_ref[...] = jnp.zeros_like(o_ref)

    hid = jnp.maximum(jnp.dot(h_ref[...], w1_ref[...], preferred_element_type=F32), 0.0)
    o_ref[...] += jnp.dot((hid * hid).astype(BF16), w2_ref[...], preferred_element_type=F32)

    @pl.when(j == pl.num_programs(1) - 1)
    def _():
        o_ref[...] = x_ref[...] + mod_ref[5:6, :] * o_ref[...]


def _mlp(x2, mod, gain, w1, w2, layer, seq):
    n, d = x2.shape
    tm, tf = MLP_TM, MLP_TF
    per_batch = seq // tm
    return pl.pallas_call(
        _mlp_kernel,
        grid=(n // tm, D_FF // tf),
        in_specs=[
            pl.BlockSpec((tm, d), lambda i, j: (i, 0)),
            pl.BlockSpec((None, 6, d), lambda i, j: (i // per_batch, 0, 0)),
            pl.BlockSpec((1, d), lambda i, j: (0, 0)),
            pl.BlockSpec((None, d, tf), lambda i, j: (layer, 0, j)),
            pl.BlockSpec((None, tf, d), lambda i, j: (layer, j, 0)),
        ],
        out_specs=pl.BlockSpec((tm, d), lambda i, j: (i, 0)),
        out_shape=jax.ShapeDtypeStruct((n, d), F32),
        scratch_shapes=[pltpu.VMEM((tm, d), BF16)],
        compiler_params=_params("arbitrary", "arbitrary"),
        name="mlp",
    )(x2, mod, gain, w1, w2)


def kernel(x, c, w_ada, b_ada, norm_gain, w_in, b_f, b_gate, q_gain, k_gain, conf_dw, conf_db,
           conf_ln_g, conf_ln_b, pool_w, pool_scale, sconv_w, w_branch, w_out, w_mlp1, w_mlp2):
    batch, seq, d = x.shape
    depth = w_ada.shape[0]
    W = WIDTH
    att_cols = 3 * W + HEADS

    w_t = jnp.swapaxes(w_in, 1, 2)
    w_packed = _repack_w_in(w_t)
    w_forget = jnp.pad(w_t[:, 3 * W:att_cols], ((0, 0), (0, LANES - HEADS), (0, 0))).astype(BF16)
    w_branch_b = w_branch.astype(BF16)
    w_out_b = (w_out.astype(BF16).reshape(depth, d, d // MERGE_TO, MERGE_TO)
               .transpose(0, 2, 1, 3))
    w_mlp1_b = w_mlp1.astype(BF16)
    w_mlp2_b = w_mlp2.astype(BF16)
    pool_w_b = pool_w.astype(BF16)
    b_gate3 = b_gate.reshape(depth, 1, 4 * d)

    mod_all = _ada(c, w_ada, b_ada).reshape(depth, batch, 6, d)
    x2 = x.reshape(batch * seq, d)
    for l in range(depth):
        mod = mod_all[l]
        b_f_row = jnp.pad(b_f[l], (0, LANES - HEADS)).reshape(1, LANES)

        z, zf, h = _inproj(x2, mod, norm_gain[l, 0:1], w_packed, w_forget, l, seq)
        y_att = _attention(z, zf, b_f_row, jnp.tile(q_gain[l], HEADS_PER_STEP).reshape(1, LANES),
                           jnp.tile(k_gain[l], HEADS_PER_STEP).reshape(1, LANES), batch, seq)
        y_rest = _conv_mixers(z, conf_dw[l], conf_db[l].reshape(1, W), conf_ln_g[l].reshape(1, W),
                              conf_ln_b[l].reshape(1, W), pool_w_b, l,
                              pool_scale[l].reshape(1, W), sconv_w[l], batch, seq)
        x2 = _merge(x2, mod, h, y_att, y_rest, w_packed, b_gate3, w_branch_b, w_out_b, l, seq)
        x2 = _mlp(x2, mod, norm_gain[l, 1:2], w_mlp1_b, w_mlp2_b, l, seq)
    return x2.reshape(batch, seq, d)
```

```python
import functools
import math

import jax
import jax.numpy as jnp
from jax import lax
from jax.experimental import pallas as pl
from jax.experimental.pallas import tpu as pltpu

F32 = jnp.float32
BF16 = jnp.bfloat16

D_MODEL = 2048
WIDTH = D_MODEL // 4
HEADS = 8
HEAD_DIM = WIDTH // HEADS
CONF_K = 31
POOL_WINDOWS = (2, 4, 8, 16)
POOL_GROUP = WIDTH // len(POOL_WINDOWS)
SHORT_K = 3
D_FF = 4 * D_MODEL
RMS_EPS = 1e-6
LN_EPS = 1e-5

LANES = 128
SUBLANES = 8
VMEM_LIMIT = 60 * 1024 * 1024

ADA_TK = 256
ADA_SPLIT = 4
INPROJ_TM = 1024
INPROJ_TN = 1536
ATT_BQ = 256
HEADS_PER_STEP = LANES // HEAD_DIM
ATT_AUG = 3
LOG2E = math.log2(math.e)
CONV_TT = 512
CONV_HALO = 32
CONV_ROWS = 32
POOL_HIST = 16
GATE_TM = 2048
GATE_TN = 256
OUTPROJ_TM = 1024
OUTPROJ_TO = 1024
MLP_TM = 1024
MLP_TF = 512

MIX_COLS = 9 * WIDTH
GATE_COL0 = MIX_COLS
PACKED_COLS = MIX_COLS + 4 * D_MODEL
ATT_COLS = 3 * WIDTH
REPACK_ROWS = 512


def _params(*sem):
    return pltpu.CompilerParams(dimension_semantics=sem, vmem_limit_bytes=VMEM_LIMIT)


def _sigmoid(x):
    return 1.0 / (1.0 + jnp.exp(-x))


def _modulated_rmsnorm(x, gain, shift, scale):
    y = x * lax.rsqrt(jnp.mean(x * x, axis=-1, keepdims=True) + RMS_EPS) * gain
    return y * (1.0 + scale) + shift


_NT = (((1,), (1,)), ((), ()))


def _repack_kernel(w_ref, tail_ref, o_ref):
    r = pl.program_id(1)
    keep = ATT_COLS // REPACK_ROWS

    @pl.when(r < keep)
    def _():
        o_ref[...] = w_ref[...].astype(BF16)

    @pl.when(r >= keep)
    def _():
        moved = jnp.concatenate([w_ref[HEADS:, :], tail_ref[...]], axis=0)
        o_ref[...] = moved.astype(BF16)


def _repack_w_in(w_t):
    depth, cols, d = w_t.shape
    tr = REPACK_ROWS
    return pl.pallas_call(
        _repack_kernel,
        grid=(depth, PACKED_COLS // tr),
        in_specs=[
            pl.BlockSpec((None, tr, d), lambda l, r: (l, r, 0)),
            pl.BlockSpec((None, HEADS, d), lambda l, r: (l, (r + 1) * (tr // HEADS), 0)),
        ],
        out_specs=pl.BlockSpec((None, tr, d), lambda l, r: (l, r, 0)),
        out_shape=jax.ShapeDtypeStruct((depth, PACKED_COLS, d), BF16),
        compiler_params=_params("arbitrary", "arbitrary"),
        name="repack",
    )(w_t, w_t)


def _ada_kernel(c_ref, *refs):
    w_refs, b_ref, o_ref = refs[:ADA_SPLIT], refs[ADA_SPLIT], refs[ADA_SPLIT + 1]

    @pl.when(pl.program_id(1) == 0)
    def _():
        o_ref[...] = jnp.broadcast_to(b_ref[...], o_ref.shape)

    cb = c_ref[...].astype(BF16)
    part = o_ref.shape[1] // ADA_SPLIT
    for s, w_ref in enumerate(w_refs):
        o_ref[:, s * part:(s + 1) * part] += jnp.dot(cb, w_ref[...].astype(BF16),
                                                     preferred_element_type=F32)


def _ada(c, w_ada, b_ada):
    depth, d, n = w_ada.shape
    b = c.shape[0]
    part = n // ADA_SPLIT
    w_spec = lambda s: pl.BlockSpec((None, ADA_TK, part), lambda l, k, s=s: (l, k, s))
    return pl.pallas_call(
        _ada_kernel,
        grid=(depth, d // ADA_TK),
        in_specs=[pl.BlockSpec((b, ADA_TK), lambda l, k: (0, k))]
        + [w_spec(s) for s in range(ADA_SPLIT)]
        + [pl.BlockSpec((None, 1, n), lambda l, k: (l, 0, 0))],
        out_specs=pl.BlockSpec((None, b, n), lambda l, k: (l, 0, 0)),
        out_shape=jax.ShapeDtypeStruct((depth, b, n), F32),
        compiler_params=_params("arbitrary", "arbitrary"),
        name="ada",
    )(c, *([w_ada] * ADA_SPLIT), b_ada.reshape(depth, 1, n))


def _inproj_kernel(x_ref, mod_ref, gain_ref, w_ref, wf_ref, z_ref, zf_ref, h_ref):
    @pl.when(pl.program_id(1) == 0)
    def _():
        h = _modulated_rmsnorm(x_ref[...], gain_ref[...], mod_ref[0:1, :], mod_ref[1:2, :])
        hb = h.astype(BF16)
        h_ref[...] = hb
        zf_ref[...] = lax.dot_general(hb, wf_ref[...], _NT, preferred_element_type=F32)

    z_ref[...] = lax.dot_general(h_ref[...], w_ref[...], _NT,
                                 preferred_element_type=F32).astype(z_ref.dtype)


def _inproj(x2, mod, gain, w_packed, w_forget, layer, seq):
    n, d = x2.shape
    tm, tn = INPROJ_TM, INPROJ_TN
    per_batch = seq // tm
    return pl.pallas_call(
        _inproj_kernel,
        grid=(n // tm, MIX_COLS // tn),
        in_specs=[
            pl.BlockSpec((tm, d), lambda i, j: (i, 0)),
            pl.BlockSpec((None, 6, d), lambda i, j: (i // per_batch, 0, 0)),
            pl.BlockSpec((1, d), lambda i, j: (0, 0)),
            pl.BlockSpec((None, tn, d), lambda i, j: (layer, j, 0)),
            pl.BlockSpec((None, LANES, d), lambda i, j: (layer, 0, 0)),
        ],
        out_specs=[
            pl.BlockSpec((tm, tn), lambda i, j: (i, j)),
            pl.BlockSpec((tm, LANES), lambda i, j: (i, 0)),
            pl.BlockSpec((tm, d), lambda i, j: (i, 0)),
        ],
        out_shape=[
            jax.ShapeDtypeStruct((n, MIX_COLS), BF16),
            jax.ShapeDtypeStruct((n, LANES), F32),
            jax.ShapeDtypeStruct((n, d), BF16),
        ],
        compiler_params=_params("arbitrary", "arbitrary"),
        name="inproj",
    )(x2, mod, gain, w_packed, w_forget)


def _attn_kernel(q_ref, k_ref, v_ref, zf_ref, bf_ref, qg_ref, kg_ref, o_ref,
                 cum_ref, qa_ref, ka_ref, s_ref, p_ref):
    seq = q_ref.shape[0]
    g = pl.program_id(1)

    @pl.when(g == 0)
    def _():
        logit = zf_ref[...] + bf_ref[...]
        log_f = jnp.minimum(logit, 0.0) - jnp.log1p(jnp.exp(-jnp.abs(logit)))
        acc = log_f.T[0:HEADS, :]
        lane = lax.broadcasted_iota(jnp.int32, acc.shape, 1)
        shift = 1
        while shift < seq:
            acc = acc + jnp.where(lane >= shift, pltpu.roll(acc, shift, axis=1), 0.0)
            shift *= 2
        pad = jnp.zeros((LANES - HEADS, seq), F32)
        rest = jnp.concatenate([acc * LOG2E, pad], axis=0).T
        for j in range(ATT_AUG):
            piece = rest.astype(BF16)
            cum_ref[:, j * LANES:(j + 1) * LANES] = piece
            rest = rest - piece.astype(F32)

    scale = 1.0 / math.sqrt(HEAD_DIM)
    lane = lax.broadcasted_iota(jnp.int32, (seq, LANES), 1)
    in_h0 = lane < HEAD_DIM

    def normalise(ref, gain_ref, mult):
        x = ref[...].astype(F32)
        sq = x * x
        ms0 = jnp.sum(jnp.where(in_h0, sq, 0.0), axis=1, keepdims=True)
        ms1 = jnp.sum(jnp.where(in_h0, 0.0, sq), axis=1, keepdims=True)
        inv = lax.rsqrt(jnp.where(in_h0, ms0, ms1) * (1.0 / HEAD_DIM) + RMS_EPS)
        return x * inv * (gain_ref[...] * mult)

    qn = normalise(q_ref, qg_ref, scale * LOG2E)
    kn = normalise(k_ref, kg_ref, 1.0)

    e_row = lax.broadcasted_iota(jnp.int32, (LANES, LANES), 0)
    e_col = lax.broadcasted_iota(jnp.int32, (LANES, LANES), 1)
    for hh in range(HEADS_PER_STEP):
        head = g * HEADS_PER_STEP + hh
        base = HEAD_DIM * (1 - hh)
        pick = e_row == head
        sel = []
        for j in range(ATT_AUG):
            plus = jnp.where(pick & (e_col == base + j), 1.0, 0.0)
            minus = jnp.where(pick & (e_col == base + ATT_AUG + j), 1.0, 0.0)
            sel.append(plus - minus)
        sel = jnp.concatenate(sel, axis=0).astype(BF16)
        aug = jnp.dot(cum_ref[...], sel, preferred_element_type=F32)
        in_head = in_h0 if hh == 0 else jnp.logical_not(in_h0)
        first = (lane >= base) & (lane < base + ATT_AUG)
        second = (lane >= base + ATT_AUG) & (lane < base + 2 * ATT_AUG)
        qa = jnp.where(in_head, qn, jnp.where(first, aug, jnp.where(second, 1.0, 0.0)))
        ka = jnp.where(in_head, kn, jnp.where(first, 1.0, jnp.where(second, aug, 0.0)))
        qa_ref[hh] = qa.astype(BF16)
        ka_ref[hh] = ka.astype(BF16)

    bq = ATT_BQ
    row = lax.broadcasted_iota(jnp.int32, (bq, bq), 0)
    col = lax.broadcasted_iota(jnp.int32, (bq, bq), 1)
    causal = row >= col
    out_h0 = lax.broadcasted_iota(jnp.int32, (bq, LANES), 1) < HEAD_DIM
    nt = (((1,), (1,)), ((), ()))
    def scores(i, hh, slot):
        qa = qa_ref[hh, i * bq:(i + 1) * bq, :]
        m_part = None
        for c in range(i + 1):
            k0 = c * bq
            s = lax.dot_general(qa, ka_ref[hh, k0:k0 + bq, :], nt, preferred_element_type=F32)
            if c == i:
                s = jnp.where(causal, s, -jnp.inf)
            s_ref[slot, :, k0:k0 + bq] = s
            for h0 in range(0, bq, LANES):
                part = s[:, h0:h0 + LANES]
                m_part = part if m_part is None else jnp.maximum(m_part, part)
        return jnp.max(m_part, axis=1, keepdims=True)

    def weighted_values(i, slot, m):
        l_part = None
        for c in range(i + 1):
            k0 = c * bq
            p = jnp.exp2(s_ref[slot, :, k0:k0 + bq] - m)
            for h0 in range(0, bq, LANES):
                part = p[:, h0:h0 + LANES]
                l_part = part if l_part is None else l_part + part
            p_ref[slot, :, k0:k0 + bq] = p.astype(BF16)
        acc = jnp.dot(p_ref[slot, :, 0:(i + 1) * bq], v_ref[0:(i + 1) * bq, :],
                      preferred_element_type=F32)
        return acc / jnp.sum(l_part, axis=1, keepdims=True)

    units = [(i, hh) for i in range(seq // bq) for hh in range(HEADS_PER_STEP)]
    m_next = scores(*units[0], 0)
    outs = []
    for u, (i, hh) in enumerate(units):
        m_cur = m_next
        if u + 1 < len(units):
            m_next = scores(*units[u + 1], (u + 1) % 2)
        outs.append(weighted_values(i, u % 2, m_cur))
        if hh == HEADS_PER_STEP - 1:
            o_ref[i * bq:(i + 1) * bq, :] = jnp.where(out_h0, outs[0], outs[1]).astype(o_ref.dtype)
            outs = []


def _attention(z, zf, b_f_row, q_gain, k_gain, batch, seq):
    n = z.shape[0]
    groups = HEADS // HEADS_PER_STEP
    blk = lambda off: pl.BlockSpec((seq, LANES), lambda b, g, off=off: (b, off + g))
    return pl.pallas_call(
        _attn_kernel,
        grid=(batch, groups),
        in_specs=[
            blk(0), blk(groups), blk(2 * groups),
            pl.BlockSpec((seq, LANES), lambda b, g: (b, 0)),
            pl.BlockSpec((1, LANES), lambda b, g: (0, 0)),
            pl.BlockSpec((1, LANES), lambda b, g: (0, 0)),
            pl.BlockSpec((1, LANES), lambda b, g: (0, 0)),
        ],
        out_specs=pl.BlockSpec((seq, LANES), lambda b, g: (b, g)),
        out_shape=jax.ShapeDtypeStruct((n, WIDTH), BF16),
        scratch_shapes=[
            pltpu.VMEM((seq, ATT_AUG * LANES), BF16),
            pltpu.VMEM((HEADS_PER_STEP, seq, LANES), BF16),
            pltpu.VMEM((HEADS_PER_STEP, seq, LANES), BF16),
            pltpu.VMEM((2, ATT_BQ, seq), F32),
            pltpu.VMEM((2, ATT_BQ, seq), BF16),
        ],
        compiler_params=_params("arbitrary", "arbitrary"),
        name="attn",
    )(z, z, z, zf, b_f_row, q_gain, k_gain)


def _conv_kernel(a_ref, g_ref, p_ref, x_ref, b_ref, c_ref,
                 ah_ref, gh_ref, ph_ref, xh_ref, ch_ref,
                 dw_ref, db_ref, lng_ref, lnb_ref, pw_ref, ps_ref, sw_ref,
                 o_ref, ubuf, pbuf, vbuf):
    tt = a_ref.shape[0]
    t = pl.program_id(1)
    has_past = t > 0
    H = CONV_HALO

    u_h = ah_ref[...].astype(F32) * _sigmoid(gh_ref[...].astype(F32))
    ubuf[0:H, :] = jnp.where(has_past, u_h, 0.0)
    ubuf[H:, :] = a_ref[...].astype(F32) * _sigmoid(g_ref[...].astype(F32))
    pbuf[0:H, :] = jnp.where(has_past, ph_ref[...].astype(F32), 0.0)
    pbuf[H:, :] = p_ref[...].astype(F32)
    v_h = ch_ref[...].astype(F32) * xh_ref[...].astype(F32)
    vbuf[0:H, :] = jnp.where(has_past, v_h, 0.0)
    vbuf[H:, :] = c_ref[...].astype(F32) * x_ref[...].astype(F32)

    R = CONV_ROWS

    def chunk(ci, carry):
        r0 = pl.multiple_of(ci * R, R)

        acc = jnp.broadcast_to(db_ref[...], (R, WIDTH))
        win = ubuf[pl.ds(r0, R + H), :]
        for r in range(SUBLANES):
            rolled = win if r == 0 else pltpu.roll(win, r, axis=0)
            for a in range((CONF_K - 1 - r) // SUBLANES + 1):
                k = CONF_K - 1 - (SUBLANES * a + r)
                lo = H - SUBLANES * a
                acc = acc + dw_ref[k:k + 1, :] * rolled[lo:lo + R, :]
        mu = jnp.mean(acc, axis=-1, keepdims=True)
        xc = acc - mu
        y = xc * lax.rsqrt(jnp.mean(xc * xc, axis=-1, keepdims=True) + LN_EPS)
        y = y * lng_ref[...] + lnb_ref[...]
        o_ref[pl.ds(r0, R), 0:WIDTH] = (y * _sigmoid(y)).astype(o_ref.dtype)

        pos = t * tt + r0 + lax.broadcasted_iota(jnp.int32, (R, 1), 0)
        pooled = []
        for gi, w in enumerate(POOL_WINDOWS):
            lo = gi * POOL_GROUP
            tot = pbuf[pl.ds(r0 + (H - POOL_HIST), R + POOL_HIST), lo:lo + POOL_GROUP]
            tok = tot[POOL_HIST:, :]
            span = 1
            while span < w:
                tot = tot + pltpu.roll(tot, span, axis=0)
                span *= 2
            cnt = jnp.minimum(pos + 1, w).astype(F32)
            dlt = (tot[POOL_HIST:, :] / cnt - tok).astype(BF16)
            pooled.append(jnp.dot(dlt, pw_ref[gi], preferred_element_type=F32))
        yp = jnp.concatenate(pooled, axis=1) * ps_ref[...]
        o_ref[pl.ds(r0, R), WIDTH:2 * WIDTH] = yp.astype(o_ref.dtype)

        vwin = vbuf[pl.ds(r0 + (H - SUBLANES), R + SUBLANES), :]
        sc = sw_ref[SHORT_K - 1:SHORT_K, :] * vwin[SUBLANES:, :]
        for dly in range(1, SHORT_K):
            k = SHORT_K - 1 - dly
            sc = sc + sw_ref[k:k + 1, :] * pltpu.roll(vwin, dly, axis=0)[SUBLANES:, :]
        ys = b_ref[pl.ds(r0, R), :].astype(F32) * sc
        o_ref[pl.ds(r0, R), 2 * WIDTH:3 * WIDTH] = ys.astype(o_ref.dtype)
        return carry

    lax.fori_loop(0, tt // R, chunk, 0, unroll=2)


def _conv_mixers(z, conf_dw, conf_db, ln_g, ln_b, pool_w, layer, pool_scale, sconv_w, batch, seq):
    n = z.shape[0]
    tt, H = CONV_TT, CONV_HALO
    tiles = seq // tt
    main = lambda cb: pl.BlockSpec((tt, WIDTH), lambda b, t, cb=cb: (b * tiles + t, cb))
    halo = lambda cb: pl.BlockSpec(
        (H, WIDTH),
        lambda b, t, cb=cb: (jnp.maximum((b * tiles + t) * (tt // H) - 1, 0), cb))
    full = lambda shape: pl.BlockSpec(shape, lambda b, t: (0,) * len(shape))
    return pl.pallas_call(
        _conv_kernel,
        grid=(batch, tiles),
        in_specs=[
            main(3), main(4), main(5), main(6), main(7), main(8),
            halo(3), halo(4), halo(5), halo(6), halo(8),
            full((CONF_K, WIDTH)), full((1, WIDTH)), full((1, WIDTH)), full((1, WIDTH)),
            pl.BlockSpec((None, len(POOL_WINDOWS), POOL_GROUP, POOL_GROUP),
                         lambda b, t: (layer, 0, 0, 0)),
            full((1, WIDTH)),
            full((SHORT_K, WIDTH)),
        ],
        out_specs=pl.BlockSpec((tt, 3 * WIDTH), lambda b, t: (b * tiles + t, 0)),
        out_shape=jax.ShapeDtypeStruct((n, 3 * WIDTH), BF16),
        scratch_shapes=[pltpu.VMEM((H + tt, WIDTH), F32)] * 3,
        compiler_params=_params("arbitrary", "arbitrary"),
        name="conv",
    )(z, z, z, z, z, z, z, z, z, z, z,
      conf_dw, conf_db, ln_g, ln_b, pool_w, pool_scale, sconv_w)


def _gate_kernel(h_ref, ya_ref, yr_ref,
                 wg0_ref, wg1_ref, wg2_ref, wg3_ref, bg0_ref, bg1_ref, bg2_ref, bg3_ref,
                 wbr_ref, o_ref):
    h = h_ref[...]
    wgs = (wg0_ref, wg1_ref, wg2_ref, wg3_ref)
    bgs = (bg0_ref, bg1_ref, bg2_ref, bg3_ref)
    merged = None
    for b in range(4):
        y = ya_ref[...] if b == 0 else yr_ref[:, (b - 1) * WIDTH:b * WIDTH]
        logit = lax.dot_general(h, wgs[b][...], _NT, preferred_element_type=F32)
        gate = _sigmoid(logit + bgs[b][...])
        term = gate * jnp.dot(y, wbr_ref[b], preferred_element_type=F32)
        merged = term if merged is None else merged + term
    o_ref[...] = merged.astype(BF16)


def _gated_merge(h, y_att, y_rest, w_packed, b_gate, w_branch, layer):
    n, d = h.shape
    tm, tn = GATE_TM, GATE_TN
    na = d // tn
    gate0 = GATE_COL0 // tn
    wg = lambda b: pl.BlockSpec((None, tn, d), lambda i, j, b=b: (layer, gate0 + b * na + j, 0))
    bg = lambda b: pl.BlockSpec((None, 1, tn), lambda i, j, b=b: (layer, 0, b * na + j))
    return pl.pallas_call(
        _gate_kernel,
        grid=(n // tm, na),
        in_specs=[
            pl.BlockSpec((tm, d), lambda i, j: (i, 0)),
            pl.BlockSpec((tm, WIDTH), lambda i, j: (i, 0)),
            pl.BlockSpec((tm, 3 * WIDTH), lambda i, j: (i, 0)),
            wg(0), wg(1), wg(2), wg(3), bg(0), bg(1), bg(2), bg(3),
            pl.BlockSpec((None, 4, WIDTH, tn), lambda i, j: (layer, 0, 0, j)),
        ],
        out_specs=pl.BlockSpec((tm, tn), lambda i, j: (i, j)),
        out_shape=jax.ShapeDtypeStruct((n, d), BF16),
        compiler_params=_params("arbitrary", "arbitrary"),
        name="gate",
    )(h, y_att, y_rest, w_packed, w_packed, w_packed, w_packed,
      b_gate, b_gate, b_gate, b_gate, w_branch)


def _outproj_kernel(m_ref, x_ref, mod_ref, wout_ref, o_ref):
    proj = jnp.dot(m_ref[...], wout_ref[pl.program_id(1)], preferred_element_type=F32)
    o_ref[...] = x_ref[...] + mod_ref[2:3, :] * proj


def _outproj(merged, x2, mod, w_out, layer, seq):
    n, d = x2.shape
    tm, to = OUTPROJ_TM, OUTPROJ_TO
    per_batch = seq // tm
    nb = d // to
    return pl.pallas_call(
        _outproj_kernel,
        grid=(n // tm, nb),
        in_specs=[
            pl.BlockSpec((tm, d), lambda i, j: (i, 0)),
            pl.BlockSpec((tm, to), lambda i, j: (i, j)),
            pl.BlockSpec((None, 6, to), lambda i, j: (i // per_batch, 0, j)),
            pl.BlockSpec((None, nb, d, to), lambda i, j: (layer, 0, 0, 0),
                         pipeline_mode=pl.Buffered(1)),
        ],
        out_specs=pl.BlockSpec((tm, to), lambda i, j: (i, j)),
        out_shape=jax.ShapeDtypeStruct((n, d), F32),
        compiler_params=_params("arbitrary", "arbitrary"),
        name="outproj",
    )(merged, x2, mod, w_out)


def _mlp_kernel(x_ref, mod_ref, gain_ref, w1_ref, w2_ref, o_ref, h_ref):
    j = pl.program_id(1)

    @pl.when(j == 0)
    def _():
        h = _modulated_rmsnorm(x_ref[...], gain_ref[...], mod_ref[3:4, :], mod_ref[4:5, :])
        h_ref[...] = h.astype(BF16)
        o_ref[...] = jnp.zeros_like(o_ref)

    hid = jnp.maximum(jnp.dot(h_ref[...], w1_ref[...], preferred_element_type=F32), 0.0)
    o_ref[...] += jnp.dot((hid * hid).astype(BF16), w2_ref[...], preferred_element_type=F32)

    @pl.when(j == pl.num_programs(1) - 1)
    def _():
        o_ref[...] = x_ref[...] + mod_ref[5:6, :] * o_ref[...]


def _mlp(x2, mod, gain, w1, w2, layer, seq):
    n, d = x2.shape
    tm, tf = MLP_TM, MLP_TF
    per_batch = seq // tm
    return pl.pallas_call(
        _mlp_kernel,
        grid=(n // tm, D_FF // tf),
        in_specs=[
            pl.BlockSpec((tm, d), lambda i, j: (i, 0)),
            pl.BlockSpec((None, 6, d), lambda i, j: (i // per_batch, 0, 0)),
            pl.BlockSpec((1, d), lambda i, j: (0, 0)),
            pl.BlockSpec((None, d, tf), lambda i, j: (layer, 0, j)),
            pl.BlockSpec((None, tf, d), lambda i, j: (layer, j, 0)),
        ],
        out_specs=pl.BlockSpec((tm, d), lambda i, j: (i, 0)),
        out_shape=jax.ShapeDtypeStruct((n, d), F32),
        scratch_shapes=[pltpu.VMEM((tm, d), BF16)],
        compiler_params=_params("arbitrary", "arbitrary"),
        name="mlp",
    )(x2, mod, gain, w1, w2)


def kernel(x, c, w_ada, b_ada, norm_gain, w_in, b_f, b_gate, q_gain, k_gain, conf_dw, conf_db,
           conf_ln_g, conf_ln_b, pool_w, pool_scale, sconv_w, w_branch, w_out, w_mlp1, w_mlp2):
    batch, seq, d = x.shape
    depth = w_ada.shape[0]
    W = WIDTH
    att_cols = 3 * W + HEADS

    w_t = jnp.swapaxes(w_in, 1, 2)
    w_packed = _repack_w_in(w_t)
    w_forget = jnp.pad(w_t[:, 3 * W:att_cols], ((0, 0), (0, LANES - HEADS), (0, 0))).astype(BF16)
    w_branch_b = w_branch.astype(BF16)
    w_out_b = (w_out.astype(BF16).reshape(depth, d, d // OUTPROJ_TO, OUTPROJ_TO)
               .transpose(0, 2, 1, 3))
    w_mlp1_b = w_mlp1.astype(BF16)
    w_mlp2_b = w_mlp2.astype(BF16)
    pool_w_b = pool_w.astype(BF16)
    b_gate3 = b_gate.reshape(depth, 1, 4 * d)

    mod_all = _ada(c, w_ada, b_ada).reshape(depth, batch, 6, d)
    x2 = x.reshape(batch * seq, d)
    for l in range(depth):
        mod = mod_all[l]
        b_f_row = jnp.pad(b_f[l], (0, LANES - HEADS)).reshape(1, LANES)

        z, zf, h = _inproj(x2, mod, norm_gain[l, 0:1], w_packed, w_forget, l, seq)
        y_att = _attention(z, zf, b_f_row, jnp.tile(q_gain[l], HEADS_PER_STEP).reshape(1, LANES),
                           jnp.tile(k_gain[l], HEADS_PER_STEP).reshape(1, LANES), batch, seq)
        y_rest = _conv_mixers(z, conf_dw[l], conf_db[l].reshape(1, W), conf_ln_g[l].reshape(1, W),
                              conf_ln_b[l].reshape(1, W), pool_w_b, l,
                              pool_scale[l].reshape(1, W), sconv_w[l], batch, seq)
        merged = _gated_merge(h, y_att, y_rest, w_packed, b_gate3, w_branch_b, l)
        x2 = _outproj(merged, x2, mod, w_out_b, l, seq)
        x2 = _mlp(x2, mod, norm_gain[l, 1:2], w_mlp1_b, w_mlp2_b, l, seq)
    return x2.reshape(batch, seq, d)
```

```python
import functools
import math

import jax
import jax.numpy as jnp
from jax import lax
from jax.experimental import pallas as pl
from jax.experimental.pallas import tpu as pltpu

F32 = jnp.float32
BF16 = jnp.bfloat16

D_MODEL = 2048
WIDTH = D_MODEL // 4
HEADS = 8
HEAD_DIM = WIDTH // HEADS
CONF_K = 31
POOL_WINDOWS = (2, 4, 8, 16)
POOL_GROUP = WIDTH // len(POOL_WINDOWS)
SHORT_K = 3
D_FF = 4 * D_MODEL
RMS_EPS = 1e-6
LN_EPS = 1e-5

LANES = 128
SUBLANES = 8
VMEM_LIMIT = 60 * 1024 * 1024

ADA_TK = 256
ADA_SPLIT = 4
INPROJ_TM = 1024
INPROJ_TN = 1536
ATT_BQ = 256
HEADS_PER_STEP = LANES // HEAD_DIM
ATT_AUG = 3
LOG2E = math.log2(math.e)
CONV_TT = 512
CONV_HALO = 32
CONV_ROWS = 32
POOL_HIST = 16
GATE_TM = 2048
GATE_TN = 256
OUTPROJ_TM = 1024
OUTPROJ_TO = 1024
MLP_TM = 1024
MLP_TF = 512

MIX_COLS = 9 * WIDTH
GATE_COL0 = MIX_COLS
PACKED_COLS = MIX_COLS + 4 * D_MODEL
ATT_COLS = 3 * WIDTH
REPACK_ROWS = 512


def _params(*sem):
    return pltpu.CompilerParams(dimension_semantics=sem, vmem_limit_bytes=VMEM_LIMIT)


def _sigmoid(x):
    return 1.0 / (1.0 + jnp.exp(-x))


def _modulated_rmsnorm(x, gain, shift, scale):
    y = x * lax.rsqrt(jnp.mean(x * x, axis=-1, keepdims=True) + RMS_EPS) * gain
    return y * (1.0 + scale) + shift


_NT = (((1,), (1,)), ((), ()))


def _repack_kernel(w_ref, tail_ref, o_ref):
    r = pl.program_id(1)
    keep = ATT_COLS // REPACK_ROWS

    @pl.when(r < keep)
    def _():
        o_ref[...] = w_ref[...].astype(BF16)

    @pl.when(r >= keep)
    def _():
        moved = jnp.concatenate([w_ref[HEADS:, :], tail_ref[...]], axis=0)
        o_ref[...] = moved.astype(BF16)


def _repack_w_in(w_t):
    depth, cols, d = w_t.shape
    tr = REPACK_ROWS
    return pl.pallas_call(
        _repack_kernel,
        grid=(depth, PACKED_COLS // tr),
        in_specs=[
            pl.BlockSpec((None, tr, d), lambda l, r: (l, r, 0)),
            pl.BlockSpec((None, HEADS, d), lambda l, r: (l, (r + 1) * (tr // HEADS), 0)),
        ],
        out_specs=pl.BlockSpec((None, tr, d), lambda l, r: (l, r, 0)),
        out_shape=jax.ShapeDtypeStruct((depth, PACKED_COLS, d), BF16),
        compiler_params=_params("arbitrary", "arbitrary"),
        name="repack",
    )(w_t, w_t)


def _ada_kernel(c_ref, *refs):
    w_refs, b_ref, o_ref = refs[:ADA_SPLIT], refs[ADA_SPLIT], refs[ADA_SPLIT + 1]

    @pl.when(pl.program_id(1) == 0)
    def _():
        o_ref[...] = jnp.broadcast_to(b_ref[...], o_ref.shape)

    cb = c_ref[...].astype(BF16)
    part = o_ref.shape[1] // ADA_SPLIT
    for s, w_ref in enumerate(w_refs):
        o_ref[:, s * part:(s + 1) * part] += jnp.dot(cb, w_ref[...].astype(BF16),
                                                     preferred_element_type=F32)


def _ada(c, w_ada, b_ada):
    depth, d, n = w_ada.shape
    b = c.shape[0]
    part = n // ADA_SPLIT
    w_spec = lambda s: pl.BlockSpec((None, ADA_TK, part), lambda l, k, s=s: (l, k, s))
    return pl.pallas_call(
        _ada_kernel,
        grid=(depth, d // ADA_TK),
        in_specs=[pl.BlockSpec((b, ADA_TK), lambda l, k: (0, k))]
        + [w_spec(s) for s in range(ADA_SPLIT)]
        + [pl.BlockSpec((None, 1, n), lambda l, k: (l, 0, 0))],
        out_specs=pl.BlockSpec((None, b, n), lambda l, k: (l, 0, 0)),
        out_shape=jax.ShapeDtypeStruct((depth, b, n), F32),
        compiler_params=_params("arbitrary", "arbitrary"),
        name="ada",
    )(c, *([w_ada] * ADA_SPLIT), b_ada.reshape(depth, 1, n))


def _inproj_kernel(x_ref, mod_ref, gain_ref, w_ref, wf_ref, z_ref, zf_ref, h_ref):
    @pl.when(pl.program_id(1) == 0)
    def _():
        h = _modulated_rmsnorm(x_ref[...], gain_ref[...], mod_ref[0:1, :], mod_ref[1:2, :])
        hb = h.astype(BF16)
        h_ref[...] = hb
        zf_ref[...] = lax.dot_general(hb, wf_ref[...], _NT, preferred_element_type=F32)

    z_ref[...] = lax.dot_general(h_ref[...], w_ref[...], _NT,
                                 preferred_element_type=F32).astype(z_ref.dtype)


def _inproj(x2, mod, gain, w_packed, w_forget, layer, seq):
    n, d = x2.shape
    tm, tn = INPROJ_TM, INPROJ_TN
    per_batch = seq // tm
    return pl.pallas_call(
        _inproj_kernel,
        grid=(n // tm, MIX_COLS // tn),
        in_specs=[
            pl.BlockSpec((tm, d), lambda i, j: (i, 0)),
            pl.BlockSpec((None, 6, d), lambda i, j: (i // per_batch, 0, 0)),
            pl.BlockSpec((1, d), lambda i, j: (0, 0)),
            pl.BlockSpec((None, tn, d), lambda i, j: (layer, j, 0)),
            pl.BlockSpec((None, LANES, d), lambda i, j: (layer, 0, 0)),
        ],
        out_specs=[
            pl.BlockSpec((tm, tn), lambda i, j: (i, j)),
            pl.BlockSpec((tm, LANES), lambda i, j: (i, 0)),
            pl.BlockSpec((tm, d), lambda i, j: (i, 0)),
        ],
        out_shape=[
            jax.ShapeDtypeStruct((n, MIX_COLS), BF16),
            jax.ShapeDtypeStruct((n, LANES), F32),
            jax.ShapeDtypeStruct((n, d), BF16),
        ],
        compiler_params=_params("arbitrary", "arbitrary"),
        name="inproj",
    )(x2, mod, gain, w_packed, w_forget)


def _attn_kernel(q_ref, k_ref, v_ref, zf_ref, bf_ref, qg_ref, kg_ref, *rest, n_cast):
    cast_in, rest = rest[:n_cast], rest[n_cast:]
    o_ref, cast_out = rest[0], rest[1:1 + n_cast]
    cum_ref, qa_ref, ka_ref, s_ref, p_ref = rest[1 + n_cast:]
    for src, dst in zip(cast_in, cast_out):
        dst[...] = src[...].astype(BF16)

    seq = q_ref.shape[0]
    g = pl.program_id(1)

    @pl.when(g == 0)
    def _():
        logit = zf_ref[...] + bf_ref[...]
        log_f = jnp.minimum(logit, 0.0) - jnp.log1p(jnp.exp(-jnp.abs(logit)))
        acc = log_f.T[0:HEADS, :]
        lane = lax.broadcasted_iota(jnp.int32, acc.shape, 1)
        shift = 1
        while shift < seq:
            acc = acc + jnp.where(lane >= shift, pltpu.roll(acc, shift, axis=1), 0.0)
            shift *= 2
        pad = jnp.zeros((LANES - HEADS, seq), F32)
        rest = jnp.concatenate([acc * LOG2E, pad], axis=0).T
        for j in range(ATT_AUG):
            piece = rest.astype(BF16)
            cum_ref[:, j * LANES:(j + 1) * LANES] = piece
            rest = rest - piece.astype(F32)

    scale = 1.0 / math.sqrt(HEAD_DIM)
    lane = lax.broadcasted_iota(jnp.int32, (seq, LANES), 1)
    in_h0 = lane < HEAD_DIM

    def normalise(ref, gain_ref, mult):
        x = ref[...].astype(F32)
        sq = x * x
        ms0 = jnp.sum(jnp.where(in_h0, sq, 0.0), axis=1, keepdims=True)
        ms1 = jnp.sum(jnp.where(in_h0, 0.0, sq), axis=1, keepdims=True)
        inv = lax.rsqrt(jnp.where(in_h0, ms0, ms1) * (1.0 / HEAD_DIM) + RMS_EPS)
        return x * inv * (gain_ref[...] * mult)

    qn = normalise(q_ref, qg_ref, scale * LOG2E)
    kn = normalise(k_ref, kg_ref, 1.0)

    e_row = lax.broadcasted_iota(jnp.int32, (LANES, LANES), 0)
    e_col = lax.broadcasted_iota(jnp.int32, (LANES, LANES), 1)
    for hh in range(HEADS_PER_STEP):
        head = g * HEADS_PER_STEP + hh
        base = HEAD_DIM * (1 - hh)
        pick = e_row == head
        sel = []
        for j in range(ATT_AUG):
            plus = jnp.where(pick & (e_col == base + j), 1.0, 0.0)
            minus = jnp.where(pick & (e_col == base + ATT_AUG + j), 1.0, 0.0)
            sel.append(plus - minus)
        sel = jnp.concatenate(sel, axis=0).astype(BF16)
        aug = jnp.dot(cum_ref[...], sel, preferred_element_type=F32)
        in_head = in_h0 if hh == 0 else jnp.logical_not(in_h0)
        first = (lane >= base) & (lane < base + ATT_AUG)
        second = (lane >= base + ATT_AUG) & (lane < base + 2 * ATT_AUG)
        qa = jnp.where(in_head, qn, jnp.where(first, aug, jnp.where(second, 1.0, 0.0)))
        ka = jnp.where(in_head, kn, jnp.where(first, 1.0, jnp.where(second, aug, 0.0)))
        qa_ref[hh] = qa.astype(BF16)
        ka_ref[hh] = ka.astype(BF16)

    bq = ATT_BQ
    row = lax.broadcasted_iota(jnp.int32, (bq, bq), 0)
    col = lax.broadcasted_iota(jnp.int32, (bq, bq), 1)
    causal = row >= col
    out_h0 = lax.broadcasted_iota(jnp.int32, (bq, LANES), 1) < HEAD_DIM
    nt = (((1,), (1,)), ((), ()))
    def scores(i, hh, slot):
        qa = qa_ref[hh, i * bq:(i + 1) * bq, :]
        m_part = None
        for c in range(i + 1):
            k0 = c * bq
            s = lax.dot_general(qa, ka_ref[hh, k0:k0 + bq, :], nt, preferred_element_type=F32)
            if c == i:
                s = jnp.where(causal, s, -jnp.inf)
            s_ref[slot, :, k0:k0 + bq] = s
            for h0 in range(0, bq, LANES):
                part = s[:, h0:h0 + LANES]
                m_part = part if m_part is None else jnp.maximum(m_part, part)
        return jnp.max(m_part, axis=1, keepdims=True)

    def weighted_values(i, slot, m):
        l_part = None
        for c in range(i + 1):
            k0 = c * bq
            p = jnp.exp2(s_ref[slot, :, k0:k0 + bq] - m)
            for h0 in range(0, bq, LANES):
                part = p[:, h0:h0 + LANES]
                l_part = part if l_part is None else l_part + part
            p_ref[slot, :, k0:k0 + bq] = p.astype(BF16)
        acc = jnp.dot(p_ref[slot, :, 0:(i + 1) * bq], v_ref[0:(i + 1) * bq, :],
                      preferred_element_type=F32)
        return acc / jnp.sum(l_part, axis=1, keepdims=True)

    units = [(i, hh) for i in range(seq // bq) for hh in range(HEADS_PER_STEP)]
    m_next = scores(*units[0], 0)
    outs = []
    for u, (i, hh) in enumerate(units):
        m_cur = m_next
        if u + 1 < len(units):
            m_next = scores(*units[u + 1], (u + 1) % 2)
        outs.append(weighted_values(i, u % 2, m_cur))
        if hh == HEADS_PER_STEP - 1:
            o_ref[i * bq:(i + 1) * bq, :] = jnp.where(out_h0, outs[0], outs[1]).astype(o_ref.dtype)
            outs = []


def _attention(z, zf, b_f_row, q_gain, k_gain, batch, seq, cast=()):
    n = z.shape[0]
    groups = HEADS // HEADS_PER_STEP
    steps = batch * groups
    blk = lambda off: pl.BlockSpec((seq, LANES), lambda b, g, off=off: (b, off + g))
    slab = lambda w: pl.BlockSpec((w.shape[0], w.shape[1] // steps, w.shape[2]),
                                  lambda b, g: (0, b * groups + g, 0))
    outs = pl.pallas_call(
        functools.partial(_attn_kernel, n_cast=len(cast)),
        grid=(batch, groups),
        in_specs=[
            blk(0), blk(groups), blk(2 * groups),
            pl.BlockSpec((seq, LANES), lambda b, g: (b, 0)),
            pl.BlockSpec((1, LANES), lambda b, g: (0, 0)),
            pl.BlockSpec((1, LANES), lambda b, g: (0, 0)),
            pl.BlockSpec((1, LANES), lambda b, g: (0, 0)),
        ] + [slab(w) for w in cast],
        out_specs=[pl.BlockSpec((seq, LANES), lambda b, g: (b, g))] + [slab(w) for w in cast],
        out_shape=[jax.ShapeDtypeStruct((n, WIDTH), BF16)]
        + [jax.ShapeDtypeStruct(w.shape, BF16) for w in cast],
        scratch_shapes=[
            pltpu.VMEM((seq, ATT_AUG * LANES), BF16),
            pltpu.VMEM((HEADS_PER_STEP, seq, LANES), BF16),
            pltpu.VMEM((HEADS_PER_STEP, seq, LANES), BF16),
            pltpu.VMEM((2, ATT_BQ, seq), F32),
            pltpu.VMEM((2, ATT_BQ, seq), BF16),
        ],
        compiler_params=_params("arbitrary", "arbitrary"),
        name="attn",
    )(z, z, z, zf, b_f_row, q_gain, k_gain, *cast)
    return outs


def _conv_kernel(a_ref, g_ref, p_ref, x_ref, b_ref, c_ref,
                 ah_ref, gh_ref, ph_ref, xh_ref, ch_ref,
                 dw_ref, db_ref, lng_ref, lnb_ref, pw_ref, ps_ref, sw_ref,
                 o_ref, ubuf, pbuf, vbuf):
    tt = a_ref.shape[0]
    t = pl.program_id(1)
    has_past = t > 0
    H = CONV_HALO

    u_h = ah_ref[...].astype(F32) * _sigmoid(gh_ref[...].astype(F32))
    ubuf[0:H, :] = jnp.where(has_past, u_h, 0.0)
    ubuf[H:, :] = a_ref[...].astype(F32) * _sigmoid(g_ref[...].astype(F32))
    pbuf[0:H, :] = jnp.where(has_past, ph_ref[...].astype(F32), 0.0)
    pbuf[H:, :] = p_ref[...].astype(F32)
    v_h = ch_ref[...].astype(F32) * xh_ref[...].astype(F32)
    vbuf[0:H, :] = jnp.where(has_past, v_h, 0.0)
    vbuf[H:, :] = c_ref[...].astype(F32) * x_ref[...].astype(F32)

    R = CONV_ROWS

    def chunk(ci, carry):
        r0 = pl.multiple_of(ci * R, R)

        acc = jnp.broadcast_to(db_ref[...], (R, WIDTH))
        win = ubuf[pl.ds(r0, R + H), :]
        for r in range(SUBLANES):
            rolled = win if r == 0 else pltpu.roll(win, r, axis=0)
            for a in range((CONF_K - 1 - r) // SUBLANES + 1):
                k = CONF_K - 1 - (SUBLANES * a + r)
                lo = H - SUBLANES * a
                acc = acc + dw_ref[k:k + 1, :] * rolled[lo:lo + R, :]
        mu = jnp.mean(acc, axis=-1, keepdims=True)
        xc = acc - mu
        y = xc * lax.rsqrt(jnp.mean(xc * xc, axis=-1, keepdims=True) + LN_EPS)
        y = y * lng_ref[...] + lnb_ref[...]
        o_ref[pl.ds(r0, R), 0:WIDTH] = (y * _sigmoid(y)).astype(o_ref.dtype)

        pos = t * tt + r0 + lax.broadcasted_iota(jnp.int32, (R, 1), 0)
        pooled = []
        for gi, w in enumerate(POOL_WINDOWS):
            lo = gi * POOL_GROUP
            tot = pbuf[pl.ds(r0 + (H - POOL_HIST), R + POOL_HIST), lo:lo + POOL_GROUP]
            tok = tot[POOL_HIST:, :]
            span = 1
            while span < w:
                tot = tot + pltpu.roll(tot, span, axis=0)
                span *= 2
            cnt = jnp.minimum(pos + 1, w).astype(F32)
            dlt = (tot[POOL_HIST:, :] / cnt - tok).astype(BF16)
            pooled.append(jnp.dot(dlt, pw_ref[gi], preferred_element_type=F32))
        yp = jnp.concatenate(pooled, axis=1) * ps_ref[...]
        o_ref[pl.ds(r0, R), WIDTH:2 * WIDTH] = yp.astype(o_ref.dtype)

        vwin = vbuf[pl.ds(r0 + (H - SUBLANES), R + SUBLANES), :]
        sc = sw_ref[SHORT_K - 1:SHORT_K, :] * vwin[SUBLANES:, :]
        for dly in range(1, SHORT_K):
            k = SHORT_K - 1 - dly
            sc = sc + sw_ref[k:k + 1, :] * pltpu.roll(vwin, dly, axis=0)[SUBLANES:, :]
        ys = b_ref[pl.ds(r0, R), :].astype(F32) * sc
        o_ref[pl.ds(r0, R), 2 * WIDTH:3 * WIDTH] = ys.astype(o_ref.dtype)
        return carry

    lax.fori_loop(0, tt // R, chunk, 0, unroll=2)


def _conv_mixers(z, conf_dw, conf_db, ln_g, ln_b, pool_w, layer, pool_scale, sconv_w, batch, seq):
    n = z.shape[0]
    tt, H = CONV_TT, CONV_HALO
    tiles = seq // tt
    main = lambda cb: pl.BlockSpec((tt, WIDTH), lambda b, t, cb=cb: (b * tiles + t, cb))
    halo = lambda cb: pl.BlockSpec(
        (H, WIDTH),
        lambda b, t, cb=cb: (jnp.maximum((b * tiles + t) * (tt // H) - 1, 0), cb))
    full = lambda shape: pl.BlockSpec(shape, lambda b, t: (0,) * len(shape))
    return pl.pallas_call(
        _conv_kernel,
        grid=(batch, tiles),
        in_specs=[
            main(3), main(4), main(5), main(6), main(7), main(8),
            halo(3), halo(4), halo(5), halo(6), halo(8),
            full((CONF_K, WIDTH)), full((1, WIDTH)), full((1, WIDTH)), full((1, WIDTH)),
            pl.BlockSpec((None, len(POOL_WINDOWS), POOL_GROUP, POOL_GROUP),
                         lambda b, t: (layer, 0, 0, 0)),
            full((1, WIDTH)),
            full((SHORT_K, WIDTH)),
        ],
        out_specs=pl.BlockSpec((tt, 3 * WIDTH), lambda b, t: (b * tiles + t, 0)),
        out_shape=jax.ShapeDtypeStruct((n, 3 * WIDTH), BF16),
        scratch_shapes=[pltpu.VMEM((H + tt, WIDTH), F32)] * 3,
        compiler_params=_params("arbitrary", "arbitrary"),
        name="conv",
    )(z, z, z, z, z, z, z, z, z, z, z,
      conf_dw, conf_db, ln_g, ln_b, pool_w, pool_scale, sconv_w)


def _gate_kernel(h_ref, ya_ref, yr_ref,
                 wg0_ref, wg1_ref, wg2_ref, wg3_ref, bg0_ref, bg1_ref, bg2_ref, bg3_ref,
                 wbr_ref, o_ref):
    h = h_ref[...]
    wgs = (wg0_ref, wg1_ref, wg2_ref, wg3_ref)
    bgs = (bg0_ref, bg1_ref, bg2_ref, bg3_ref)
    merged = None
    for b in range(4):
        y = ya_ref[...] if b == 0 else yr_ref[:, (b - 1) * WIDTH:b * WIDTH]
        logit = lax.dot_general(h, wgs[b][...], _NT, preferred_element_type=F32)
        gate = _sigmoid(logit + bgs[b][...])
        term = gate * jnp.dot(y, wbr_ref[b], preferred_element_type=F32)
        merged = term if merged is None else merged + term
    o_ref[...] = merged.astype(BF16)


def _gated_merge(h, y_att, y_rest, w_packed, b_gate, w_branch, layer):
    n, d = h.shape
    tm, tn = GATE_TM, GATE_TN
    na = d // tn
    gate0 = GATE_COL0 // tn
    wg = lambda b: pl.BlockSpec((None, tn, d), lambda i, j, b=b: (layer, gate0 + b * na + j, 0))
    bg = lambda b: pl.BlockSpec((None, 1, tn), lambda i, j, b=b: (layer, 0, b * na + j))
    return pl.pallas_call(
        _gate_kernel,
        grid=(n // tm, na),
        in_specs=[
            pl.BlockSpec((tm, d), lambda i, j: (i, 0)),
            pl.BlockSpec((tm, WIDTH), lambda i, j: (i, 0)),
            pl.BlockSpec((tm, 3 * WIDTH), lambda i, j: (i, 0)),
            wg(0), wg(1), wg(2), wg(3), bg(0), bg(1), bg(2), bg(3),
            pl.BlockSpec((None, 4, WIDTH, tn), lambda i, j: (layer, 0, 0, j)),
        ],
        out_specs=pl.BlockSpec((tm, tn), lambda i, j: (i, j)),
        out_shape=jax.ShapeDtypeStruct((n, d), BF16),
        compiler_params=_params("arbitrary", "arbitrary"),
        name="gate",
    )(h, y_att, y_rest, w_packed, w_packed, w_packed, w_packed,
      b_gate, b_gate, b_gate, b_gate, w_branch)


def _outproj_kernel(m_ref, x_ref, mod_ref, wout_ref, o_ref):
    proj = jnp.dot(m_ref[...], wout_ref[pl.program_id(1)], preferred_element_type=F32)
    o_ref[...] = x_ref[...] + mod_ref[2:3, :] * proj


def _outproj(merged, x2, mod, w_out, layer, seq):
    n, d = x2.shape
    tm, to = OUTPROJ_TM, OUTPROJ_TO
    per_batch = seq // tm
    nb = d // to
    return pl.pallas_call(
        _outproj_kernel,
        grid=(n // tm, nb),
        in_specs=[
            pl.BlockSpec((tm, d), lambda i, j: (i, 0)),
            pl.BlockSpec((tm, to), lambda i, j: (i, j)),
            pl.BlockSpec((None, 6, to), lambda i, j: (i // per_batch, 0, j)),
            pl.BlockSpec((None, nb, d, to), lambda i, j: (layer, 0, 0, 0),
                         pipeline_mode=pl.Buffered(1)),
        ],
        out_specs=pl.BlockSpec((tm, to), lambda i, j: (i, j)),
        out_shape=jax.ShapeDtypeStruct((n, d), F32),
        compiler_params=_params("arbitrary", "arbitrary"),
        name="outproj",
    )(merged, x2, mod, w_out)


def _mlp_kernel(x_ref, mod_ref, gain_ref, w1_ref, w2_ref, o_ref, h_ref):
    j = pl.program_id(1)

    @pl.when(j == 0)
    def _():
        x = x_ref[...]
        h = _modulated_rmsnorm(x, gain_ref[...], mod_ref[3:4, :], mod_ref[4:5, :])
        h_ref[...] = h.astype(BF16)
        o_ref[...] = x

    hid = jnp.maximum(jnp.dot(h_ref[...], w1_ref[...], preferred_element_type=F32), 0.0)
    part = jnp.dot((hid * hid).astype(BF16), w2_ref[...], preferred_element_type=F32)
    o_ref[...] += mod_ref[5:6, :] * part


def _mlp(x2, mod, gain, w1, w2, layer, seq):
    n, d = x2.shape
    tm, tf = MLP_TM, MLP_TF
    per_batch = seq // tm
    return pl.pallas_call(
        _mlp_kernel,
        grid=(n // tm, D_FF // tf),
        in_specs=[
            pl.BlockSpec((tm, d), lambda i, j: (i, 0)),
            pl.BlockSpec((None, 6, d), lambda i, j: (i // per_batch, 0, 0)),
            pl.BlockSpec((1, d), lambda i, j: (0, 0)),
            pl.BlockSpec((None, d, tf), lambda i, j: (layer, 0, j)),
            pl.BlockSpec((None, tf, d), lambda i, j: (layer, j, 0)),
        ],
        out_specs=pl.BlockSpec((tm, d), lambda i, j: (i, 0)),
        out_shape=jax.ShapeDtypeStruct((n, d), F32),
        scratch_shapes=[pltpu.VMEM((tm, d), BF16)],
        compiler_params=_params("arbitrary", "arbitrary"),
        name="mlp",
    )(x2, mod, gain, w1, w2)


def kernel(x, c, w_ada, b_ada, norm_gain, w_in, b_f, b_gate, q_gain, k_gain, conf_dw, conf_db,
           conf_ln_g, conf_ln_b, pool_w, pool_scale, sconv_w, w_branch, w_out, w_mlp1, w_mlp2):
    batch, seq, d = x.shape
    depth = w_ada.shape[0]
    W = WIDTH
    att_cols = 3 * W + HEADS

    w_t = jnp.swapaxes(w_in, 1, 2)
    w_packed = _repack_w_in(w_t)
    w_forget = jnp.pad(w_t[:, 3 * W:att_cols], ((0, 0), (0, LANES - HEADS), (0, 0))).astype(BF16)
    w_branch_b = w_branch.astype(BF16)
    w_out_b = (w_out.astype(BF16).reshape(depth, d, d // OUTPROJ_TO, OUTPROJ_TO)
               .transpose(0, 2, 1, 3))
    pool_w_b = pool_w.astype(BF16)
    b_gate3 = b_gate.reshape(depth, 1, 4 * d)

    mod_all = _ada(c, w_ada, b_ada).reshape(depth, batch, 6, d)
    x2 = x.reshape(batch * seq, d)
    for l in range(depth):
        mod = mod_all[l]
        b_f_row = jnp.pad(b_f[l], (0, LANES - HEADS)).reshape(1, LANES)

        z, zf, h = _inproj(x2, mod, norm_gain[l, 0:1], w_packed, w_forget, l, seq)
        y_att, *cast = _attention(z, zf, b_f_row,
                                  jnp.tile(q_gain[l], HEADS_PER_STEP).reshape(1, LANES),
                                  jnp.tile(k_gain[l], HEADS_PER_STEP).reshape(1, LANES), batch, seq,
                                  cast=(w_mlp1, w_mlp2) if l == 0 else ())
        if cast:
            w_mlp1_b, w_mlp2_b = cast
        y_rest = _conv_mixers(z, conf_dw[l], conf_db[l].reshape(1, W), conf_ln_g[l].reshape(1, W),
                              conf_ln_b[l].reshape(1, W), pool_w_b, l,
                              pool_scale[l].reshape(1, W), sconv_w[l], batch, seq)
        merged = _gated_merge(h, y_att, y_rest, w_packed, b_gate3, w_branch_b, l)
        x2 = _outproj(merged, x2, mod, w_out_b, l, seq)
        x2 = _mlp(x2, mod, norm_gain[l, 1:2], w_mlp1_b, w_mlp2_b, l, seq)
    return x2.reshape(batch, seq, d)
```

```python
import functools
import math

import jax
import jax.numpy as jnp
from jax import lax
from jax.experimental import pallas as pl
from jax.experimental.pallas import tpu as pltpu

F32 = jnp.float32
BF16 = jnp.bfloat16

D_MODEL = 2048
WIDTH = D_MODEL // 4
HEADS = 8
HEAD_DIM = WIDTH // HEADS
CONF_K = 31
POOL_WINDOWS = (2, 4, 8, 16)
POOL_GROUP = WIDTH // len(POOL_WINDOWS)
SHORT_K = 3
D_FF = 4 * D_MODEL
RMS_EPS = 1e-6
LN_EPS = 1e-5

LANES = 128
SUBLANES = 8
VMEM_LIMIT = 60 * 1024 * 1024

ADA_TK = 256
ADA_SPLIT = 4
INPROJ_TM = 1024
INPROJ_TN = 1536
ATT_BQ = 256
HEADS_PER_STEP = LANES // HEAD_DIM
ATT_AUG = 3
LOG2E = math.log2(math.e)
CONV_TT = 512
CONV_HALO = 32
CONV_ROWS = 32
POOL_HIST = 16
GATE_TM = 2048
GATE_TN = 256
OUTPROJ_TM = 1024
MLP_TM = 1024
MLP_TF = 512

MIX_COLS = 9 * WIDTH
GATE_COL0 = MIX_COLS
PACKED_COLS = MIX_COLS + 4 * D_MODEL
ATT_COLS = 3 * WIDTH
REPACK_ROWS = 512


def _params(*sem):
    return pltpu.CompilerParams(dimension_semantics=sem, vmem_limit_bytes=VMEM_LIMIT)


def _sigmoid(x):
    return 1.0 / (1.0 + jnp.exp(-x))


def _modulated_rmsnorm(x, gain, shift, scale):
    y = x * lax.rsqrt(jnp.mean(x * x, axis=-1, keepdims=True) + RMS_EPS) * gain
    return y * (1.0 + scale) + shift


_NT = (((1,), (1,)), ((), ()))


def _repacked_rows(main, tail, first_row):
    shifted = jnp.concatenate([main[HEADS:, :], tail], axis=0)
    row = first_row + lax.broadcasted_iota(jnp.int32, (main.shape[0], 1), 0)
    return jnp.where(row >= ATT_COLS, shifted, main).astype(BF16)


def _repack_specs(layer, rows, index):
    main = pl.BlockSpec((None, rows, D_MODEL), lambda *g: (layer, index(*g), 0))
    tail = pl.BlockSpec((None, HEADS, D_MODEL),
                        lambda *g: (layer, (index(*g) + 1) * (rows // HEADS), 0))
    out = pl.BlockSpec((None, rows, D_MODEL), lambda *g: (0, index(*g), 0))
    return [main, tail], out


def _repack_kernel(w_ref, tail_ref, o_ref):
    o_ref[...] = _repacked_rows(w_ref[...], tail_ref[...], pl.program_id(0) * w_ref.shape[0])


def _repack_w_in(w_t, layer):
    in_specs, out_spec = _repack_specs(layer, REPACK_ROWS, lambda r: r)
    return pl.pallas_call(
        _repack_kernel,
        grid=(PACKED_COLS // REPACK_ROWS,),
        in_specs=in_specs,
        out_specs=out_spec,
        out_shape=jax.ShapeDtypeStruct((1, PACKED_COLS, w_t.shape[2]), BF16),
        compiler_params=_params("arbitrary"),
        name="repack",
    )(w_t, w_t)


def _ada_kernel(c_ref, *refs):
    w_refs, b_ref, o_ref = refs[:ADA_SPLIT], refs[ADA_SPLIT], refs[ADA_SPLIT + 1]

    @pl.when(pl.program_id(1) == 0)
    def _():
        o_ref[...] = jnp.broadcast_to(b_ref[...], o_ref.shape)

    cb = c_ref[...].astype(BF16)
    part = o_ref.shape[1] // ADA_SPLIT
    for s, w_ref in enumerate(w_refs):
        o_ref[:, s * part:(s + 1) * part] += jnp.dot(cb, w_ref[...].astype(BF16),
                                                     preferred_element_type=F32)


def _ada(c, w_ada, b_ada):
    depth, d, n = w_ada.shape
    b = c.shape[0]
    part = n // ADA_SPLIT
    w_spec = lambda s: pl.BlockSpec((None, ADA_TK, part), lambda l, k, s=s: (l, k, s))
    return pl.pallas_call(
        _ada_kernel,
        grid=(depth, d // ADA_TK),
        in_specs=[pl.BlockSpec((b, ADA_TK), lambda l, k: (0, k))]
        + [w_spec(s) for s in range(ADA_SPLIT)]
        + [pl.BlockSpec((None, 1, n), lambda l, k: (l, 0, 0))],
        out_specs=pl.BlockSpec((None, b, n), lambda l, k: (l, 0, 0)),
        out_shape=jax.ShapeDtypeStruct((depth, b, n), F32),
        compiler_params=_params("arbitrary", "arbitrary"),
        name="ada",
    )(c, *([w_ada] * ADA_SPLIT), b_ada.reshape(depth, 1, n))


def _inproj_kernel(x_ref, mod_ref, gain_ref, w_ref, wf_ref, z_ref, zf_ref, h_ref):
    @pl.when(pl.program_id(1) == 0)
    def _():
        h = _modulated_rmsnorm(x_ref[...], gain_ref[...], mod_ref[0:1, :], mod_ref[1:2, :])
        hb = h.astype(BF16)
        h_ref[...] = hb
        zf_ref[...] = lax.dot_general(hb, wf_ref[...], _NT, preferred_element_type=F32)

    z_ref[...] = lax.dot_general(h_ref[...], w_ref[...], _NT,
                                 preferred_element_type=F32).astype(z_ref.dtype)


def _inproj(x2, mod, gain, w_packed, w_forget, layer, seq):
    n, d = x2.shape
    tm, tn = INPROJ_TM, INPROJ_TN
    per_batch = seq // tm
    return pl.pallas_call(
        _inproj_kernel,
        grid=(n // tm, MIX_COLS // tn),
        in_specs=[
            pl.BlockSpec((tm, d), lambda i, j: (i, 0)),
            pl.BlockSpec((None, 6, d), lambda i, j: (i // per_batch, 0, 0)),
            pl.BlockSpec((1, d), lambda i, j: (0, 0)),
            pl.BlockSpec((None, tn, d), lambda i, j: (0, j, 0)),
            pl.BlockSpec((None, LANES, d), lambda i, j: (layer, 0, 0)),
        ],
        out_specs=[
            pl.BlockSpec((tm, tn), lambda i, j: (i, j)),
            pl.BlockSpec((tm, LANES), lambda i, j: (i, 0)),
            pl.BlockSpec((tm, d), lambda i, j: (i, 0)),
        ],
        out_shape=[
            jax.ShapeDtypeStruct((n, MIX_COLS), BF16),
            jax.ShapeDtypeStruct((n, LANES), F32),
            jax.ShapeDtypeStruct((n, d), BF16),
        ],
        compiler_params=_params("arbitrary", "arbitrary"),
        name="inproj",
    )(x2, mod, gain, w_packed, w_forget)


def _attn_kernel(q_ref, k_ref, v_ref, zf_ref, bf_ref, qg_ref, kg_ref, *rest, n_cast, n_repack):
    n_in = n_cast + 2 * n_repack
    side_in, rest = rest[:n_in], rest[n_in:]
    o_ref, side_out = rest[0], rest[1:1 + n_cast + n_repack]
    cum_ref, qa_ref, ka_ref, s_ref, p_ref = rest[1 + n_cast + n_repack:]
    for src, dst in zip(side_in[:n_cast], side_out[:n_cast]):
        dst[...] = src[...].astype(BF16)
    if n_repack:
        main_ref, tail_ref, packed_ref = side_in[n_cast], side_in[n_cast + 1], side_out[n_cast]
        step = pl.program_id(0) * pl.num_programs(1) + pl.program_id(1)
        packed_ref[...] = _repacked_rows(main_ref[...], tail_ref[...], step * main_ref.shape[0])

    seq = q_ref.shape[0]
    g = pl.program_id(1)

    @pl.when(g == 0)
    def _():
        logit = zf_ref[...] + bf_ref[...]
        log_f = jnp.minimum(logit, 0.0) - jnp.log1p(jnp.exp(-jnp.abs(logit)))
        acc = log_f.T[0:HEADS, :]
        lane = lax.broadcasted_iota(jnp.int32, acc.shape, 1)
        shift = 1
        while shift < seq:
            acc = acc + jnp.where(lane >= shift, pltpu.roll(acc, shift, axis=1), 0.0)
            shift *= 2
        pad = jnp.zeros((LANES - HEADS, seq), F32)
        rest = jnp.concatenate([acc * LOG2E, pad], axis=0).T
        for j in range(ATT_AUG):
            piece = rest.astype(BF16)
            cum_ref[:, j * LANES:(j + 1) * LANES] = piece
            rest = rest - piece.astype(F32)

    scale = 1.0 / math.sqrt(HEAD_DIM)
    lane = lax.broadcasted_iota(jnp.int32, (seq, LANES), 1)
    in_h0 = lane < HEAD_DIM

    def normalise(ref, gain_ref, mult):
        x = ref[...].astype(F32)
        sq = x * x
        ms0 = jnp.sum(jnp.where(in_h0, sq, 0.0), axis=1, keepdims=True)
        ms1 = jnp.sum(jnp.where(in_h0, 0.0, sq), axis=1, keepdims=True)
        inv = lax.rsqrt(jnp.where(in_h0, ms0, ms1) * (1.0 / HEAD_DIM) + RMS_EPS)
        return x * inv * (gain_ref[...] * mult)

    qn = normalise(q_ref, qg_ref, scale * LOG2E)
    kn = normalise(k_ref, kg_ref, 1.0)

    e_row = lax.broadcasted_iota(jnp.int32, (LANES, LANES), 0)
    e_col = lax.broadcasted_iota(jnp.int32, (LANES, LANES), 1)
    for hh in range(HEADS_PER_STEP):
        head = g * HEADS_PER_STEP + hh
        base = HEAD_DIM * (1 - hh)
        pick = e_row == head
        sel = []
        for j in range(ATT_AUG):
            plus = jnp.where(pick & (e_col == base + j), 1.0, 0.0)
            minus = jnp.where(pick & (e_col == base + ATT_AUG + j), 1.0, 0.0)
            sel.append(plus - minus)
        sel = jnp.concatenate(sel, axis=0).astype(BF16)
        aug = jnp.dot(cum_ref[...], sel, preferred_element_type=F32)
        in_head = in_h0 if hh == 0 else jnp.logical_not(in_h0)
        first = (lane >= base) & (lane < base + ATT_AUG)
        second = (lane >= base + ATT_AUG) & (lane < base + 2 * ATT_AUG)
        qa = jnp.where(in_head, qn, jnp.where(first, aug, jnp.where(second, 1.0, 0.0)))
        ka = jnp.where(in_head, kn, jnp.where(first, 1.0, jnp.where(second, aug, 0.0)))
        qa_ref[hh] = qa.astype(BF16)
        ka_ref[hh] = ka.astype(BF16)

    bq = ATT_BQ
    row = lax.broadcasted_iota(jnp.int32, (bq, bq), 0)
    col = lax.broadcasted_iota(jnp.int32, (bq, bq), 1)
    causal = row >= col
    out_h0 = lax.broadcasted_iota(jnp.int32, (bq, LANES), 1) < HEAD_DIM
    nt = (((1,), (1,)), ((), ()))
    def scores(i, hh, slot):
        qa = qa_ref[hh, i * bq:(i + 1) * bq, :]
        m_part = None
        for c in range(i + 1):
            k0 = c * bq
            s = lax.dot_general(qa, ka_ref[hh, k0:k0 + bq, :], nt, preferred_element_type=F32)
            if c == i:
                s = jnp.where(causal, s, -jnp.inf)
            s_ref[slot, :, k0:k0 + bq] = s
            for h0 in range(0, bq, LANES):
                part = s[:, h0:h0 + LANES]
                m_part = part if m_part is None else jnp.maximum(m_part, part)
        return jnp.max(m_part, axis=1, keepdims=True)

    def weighted_values(i, slot, m):
        l_part = None
        for c in range(i + 1):
            k0 = c * bq
            p = jnp.exp2(s_ref[slot, :, k0:k0 + bq] - m)
            for h0 in range(0, bq, LANES):
                part = p[:, h0:h0 + LANES]
                l_part = part if l_part is None else l_part + part
            p_ref[slot, :, k0:k0 + bq] = p.astype(BF16)
        acc = jnp.dot(p_ref[slot, :, 0:(i + 1) * bq], v_ref[0:(i + 1) * bq, :],
                      preferred_element_type=F32)
        return acc / jnp.sum(l_part, axis=1, keepdims=True)

    units = [(i, hh) for i in range(seq // bq) for hh in range(HEADS_PER_STEP)]
    m_next = scores(*units[0], 0)
    outs = []
    for u, (i, hh) in enumerate(units):
        m_cur = m_next
        if u + 1 < len(units):
            m_next = scores(*units[u + 1], (u + 1) % 2)
        outs.append(weighted_values(i, u % 2, m_cur))
        if hh == HEADS_PER_STEP - 1:
            o_ref[i * bq:(i + 1) * bq, :] = jnp.where(out_h0, outs[0], outs[1]).astype(o_ref.dtype)
            outs = []


def _attention(z, zf, b_f_row, q_gain, k_gain, batch, seq, cast=(), cast_layer=0, repack=None):
    n = z.shape[0]
    groups = HEADS // HEADS_PER_STEP
    steps = batch * groups
    blk = lambda off: pl.BlockSpec((seq, LANES), lambda b, g, off=off: (b, off + g))
    slab = lambda w, layer: pl.BlockSpec((1, w.shape[1] // steps, w.shape[2]),
                                         lambda b, g: (layer, b * groups + g, 0))
    side_in, side_specs_in = list(cast), [slab(w, cast_layer) for w in cast]
    side_specs_out = [slab(w, 0) for w in cast]
    side_shapes = [jax.ShapeDtypeStruct((1,) + w.shape[1:], BF16) for w in cast]
    if repack is not None:
        w_t, layer = repack
        specs_in, spec_out = _repack_specs(layer, PACKED_COLS // steps, lambda b, g: b * groups + g)
        side_in += [w_t, w_t]
        side_specs_in += specs_in
        side_specs_out.append(spec_out)
        side_shapes.append(jax.ShapeDtypeStruct((1, PACKED_COLS, w_t.shape[2]), BF16))
    outs = pl.pallas_call(
        functools.partial(_attn_kernel, n_cast=len(cast), n_repack=int(repack is not None)),
        grid=(batch, groups),
        in_specs=[
            blk(0), blk(groups), blk(2 * groups),
            pl.BlockSpec((seq, LANES), lambda b, g: (b, 0)),
            pl.BlockSpec((1, LANES), lambda b, g: (0, 0)),
            pl.BlockSpec((1, LANES), lambda b, g: (0, 0)),
            pl.BlockSpec((1, LANES), lambda b, g: (0, 0)),
        ] + side_specs_in,
        out_specs=[pl.BlockSpec((seq, LANES), lambda b, g: (b, g))] + side_specs_out,
        out_shape=[jax.ShapeDtypeStruct((n, WIDTH), BF16)] + side_shapes,
        scratch_shapes=[
            pltpu.VMEM((seq, ATT_AUG * LANES), BF16),
            pltpu.VMEM((HEADS_PER_STEP, seq, LANES), BF16),
            pltpu.VMEM((HEADS_PER_STEP, seq, LANES), BF16),
            pltpu.VMEM((2, ATT_BQ, seq), F32),
            pltpu.VMEM((2, ATT_BQ, seq), BF16),
        ],
        compiler_params=_params("arbitrary", "arbitrary"),
        name="attn",
    )(z, z, z, zf, b_f_row, q_gain, k_gain, *side_in)
    return outs


def _conv_kernel(a_ref, g_ref, p_ref, x_ref, b_ref, c_ref,
                 ah_ref, gh_ref, ph_ref, xh_ref, ch_ref,
                 dw_ref, db_ref, lng_ref, lnb_ref, pw_ref, ps_ref, sw_ref,
                 o_ref, ubuf, pbuf, vbuf):
    tt = a_ref.shape[0]
    t = pl.program_id(1)
    has_past = t > 0
    H = CONV_HALO

    u_h = ah_ref[...].astype(F32) * _sigmoid(gh_ref[...].astype(F32))
    ubuf[0:H, :] = jnp.where(has_past, u_h, 0.0)
    ubuf[H:, :] = a_ref[...].astype(F32) * _sigmoid(g_ref[...].astype(F32))
    pbuf[0:H, :] = jnp.where(has_past, ph_ref[...].astype(F32), 0.0)
    pbuf[H:, :] = p_ref[...].astype(F32)
    v_h = ch_ref[...].astype(F32) * xh_ref[...].astype(F32)
    vbuf[0:H, :] = jnp.where(has_past, v_h, 0.0)
    vbuf[H:, :] = c_ref[...].astype(F32) * x_ref[...].astype(F32)

    R = CONV_ROWS

    def chunk(ci, carry):
        r0 = pl.multiple_of(ci * R, R)

        acc = jnp.broadcast_to(db_ref[...], (R, WIDTH))
        win = ubuf[pl.ds(r0, R + H), :]
        for r in range(SUBLANES):
            rolled = win if r == 0 else pltpu.roll(win, r, axis=0)
            for a in range((CONF_K - 1 - r) // SUBLANES + 1):
                k = CONF_K - 1 - (SUBLANES * a + r)
                lo = H - SUBLANES * a
                acc = acc + dw_ref[k:k + 1, :] * rolled[lo:lo + R, :]
        mu = jnp.mean(acc, axis=-1, keepdims=True)
        xc = acc - mu
        y = xc * lax.rsqrt(jnp.mean(xc * xc, axis=-1, keepdims=True) + LN_EPS)
        y = y * lng_ref[...] + lnb_ref[...]
        o_ref[pl.ds(r0, R), 0:WIDTH] = (y * _sigmoid(y)).astype(o_ref.dtype)

        pos = t * tt + r0 + lax.broadcasted_iota(jnp.int32, (R, 1), 0)
        pooled = []
        for gi, w in enumerate(POOL_WINDOWS):
            lo = gi * POOL_GROUP
            tot = pbuf[pl.ds(r0 + (H - POOL_HIST), R + POOL_HIST), lo:lo + POOL_GROUP]
            tok = tot[POOL_HIST:, :]
            span = 1
            while span < w:
                tot = tot + pltpu.roll(tot, span, axis=0)
                span *= 2
            cnt = jnp.minimum(pos + 1, w).astype(F32)
            dlt = (tot[POOL_HIST:, :] / cnt - tok).astype(BF16)
            pooled.append(jnp.dot(dlt, pw_ref[gi], preferred_element_type=F32))
        yp = jnp.concatenate(pooled, axis=1) * ps_ref[...]
        o_ref[pl.ds(r0, R), WIDTH:2 * WIDTH] = yp.astype(o_ref.dtype)

        vwin = vbuf[pl.ds(r0 + (H - SUBLANES), R + SUBLANES), :]
        sc = sw_ref[SHORT_K - 1:SHORT_K, :] * vwin[SUBLANES:, :]
        for dly in range(1, SHORT_K):
            k = SHORT_K - 1 - dly
            sc = sc + sw_ref[k:k + 1, :] * pltpu.roll(vwin, dly, axis=0)[SUBLANES:, :]
        ys = b_ref[pl.ds(r0, R), :].astype(F32) * sc
        o_ref[pl.ds(r0, R), 2 * WIDTH:3 * WIDTH] = ys.astype(o_ref.dtype)
        return carry

    lax.fori_loop(0, tt // R, chunk, 0, unroll=2)


def _conv_mixers(z, conf_dw, conf_db, ln_g, ln_b, pool_w, layer, pool_scale, sconv_w, batch, seq):
    n = z.shape[0]
    tt, H = CONV_TT, CONV_HALO
    tiles = seq // tt
    main = lambda cb: pl.BlockSpec((tt, WIDTH), lambda b, t, cb=cb: (b * tiles + t, cb))
    halo = lambda cb: pl.BlockSpec(
        (H, WIDTH),
        lambda b, t, cb=cb: (jnp.maximum((b * tiles + t) * (tt // H) - 1, 0), cb))
    full = lambda shape: pl.BlockSpec(shape, lambda b, t: (0,) * len(shape))
    return pl.pallas_call(
        _conv_kernel,
        grid=(batch, tiles),
        in_specs=[
            main(3), main(4), main(5), main(6), main(7), main(8),
            halo(3), halo(4), halo(5), halo(6), halo(8),
            full((CONF_K, WIDTH)), full((1, WIDTH)), full((1, WIDTH)), full((1, WIDTH)),
            pl.BlockSpec((None, len(POOL_WINDOWS), POOL_GROUP, POOL_GROUP),
                         lambda b, t: (layer, 0, 0, 0)),
            full((1, WIDTH)),
            full((SHORT_K, WIDTH)),
        ],
        out_specs=pl.BlockSpec((tt, 3 * WIDTH), lambda b, t: (b * tiles + t, 0)),
        out_shape=jax.ShapeDtypeStruct((n, 3 * WIDTH), BF16),
        scratch_shapes=[pltpu.VMEM((H + tt, WIDTH), F32)] * 3,
        compiler_params=_params("arbitrary", "arbitrary"),
        name="conv",
    )(z, z, z, z, z, z, z, z, z, z, z,
      conf_dw, conf_db, ln_g, ln_b, pool_w, pool_scale, sconv_w)


def _gate_kernel(h_ref, ya_ref, yr_ref,
                 wg0_ref, wg1_ref, wg2_ref, wg3_ref, bg0_ref, bg1_ref, bg2_ref, bg3_ref,
                 wbr_ref, o_ref):
    h = h_ref[...]
    wgs = (wg0_ref, wg1_ref, wg2_ref, wg3_ref)
    bgs = (bg0_ref, bg1_ref, bg2_ref, bg3_ref)
    merged = None
    for b in range(4):
        y = ya_ref[...] if b == 0 else yr_ref[:, (b - 1) * WIDTH:b * WIDTH]
        logit = lax.dot_general(h, wgs[b][...], _NT, preferred_element_type=F32)
        gate = _sigmoid(logit + bgs[b][...])
        term = gate * jnp.dot(y, wbr_ref[b], preferred_element_type=F32)
        merged = term if merged is None else merged + term
    o_ref[...] = merged.astype(BF16)


def _gated_merge(h, y_att, y_rest, w_packed, b_gate, w_branch, layer):
    n, d = h.shape
    tm, tn = GATE_TM, GATE_TN
    na = d // tn
    gate0 = GATE_COL0 // tn
    wg = lambda b: pl.BlockSpec((None, tn, d), lambda i, j, b=b: (0, gate0 + b * na + j, 0))
    bg = lambda b: pl.BlockSpec((None, 1, tn), lambda i, j, b=b: (layer, 0, b * na + j))
    return pl.pallas_call(
        _gate_kernel,
        grid=(n // tm, na),
        in_specs=[
            pl.BlockSpec((tm, d), lambda i, j: (i, 0)),
            pl.BlockSpec((tm, WIDTH), lambda i, j: (i, 0)),
            pl.BlockSpec((tm, 3 * WIDTH), lambda i, j: (i, 0)),
            wg(0), wg(1), wg(2), wg(3), bg(0), bg(1), bg(2), bg(3),
            pl.BlockSpec((None, 4, WIDTH, tn), lambda i, j: (0, 0, 0, j)),
        ],
        out_specs=pl.BlockSpec((tm, tn), lambda i, j: (i, j)),
        out_shape=jax.ShapeDtypeStruct((n, d), BF16),
        compiler_params=_params("arbitrary", "arbitrary"),
        name="gate",
    )(h, y_att, y_rest, w_packed, w_packed, w_packed, w_packed,
      b_gate, b_gate, b_gate, b_gate, w_branch)


def _outproj_kernel(m_ref, x_ref, mod_ref, wout_ref, o_ref):
    proj = jnp.dot(m_ref[...], wout_ref[...], preferred_element_type=F32)
    o_ref[...] = x_ref[...] + mod_ref[2:3, :] * proj


def _outproj(merged, x2, mod, w_out, seq):
    n, d = x2.shape
    tm = OUTPROJ_TM
    per_batch = seq // tm
    return pl.pallas_call(
        _outproj_kernel,
        grid=(n // tm,),
        in_specs=[
            pl.BlockSpec((tm, d), lambda i: (i, 0)),
            pl.BlockSpec((tm, d), lambda i: (i, 0)),
            pl.BlockSpec((None, 6, d), lambda i: (i // per_batch, 0, 0)),
            pl.BlockSpec((None, d, d), lambda i: (0, 0, 0), pipeline_mode=pl.Buffered(1)),
        ],
        out_specs=pl.BlockSpec((tm, d), lambda i: (i, 0)),
        out_shape=jax.ShapeDtypeStruct((n, d), F32),
        compiler_params=_params("arbitrary"),
        name="outproj",
    )(merged, x2, mod, w_out)


def _mlp_kernel(x_ref, mod_ref, gain_ref, w1_ref, w2_ref, o_ref, h_ref):
    j = pl.program_id(1)

    @pl.when(j == 0)
    def _():
        x = x_ref[...]
        h = _modulated_rmsnorm(x, gain_ref[...], mod_ref[3:4, :], mod_ref[4:5, :])
        h_ref[...] = h.astype(BF16)
        o_ref[...] = x

    hid = jnp.maximum(jnp.dot(h_ref[...], w1_ref[...], preferred_element_type=F32), 0.0)
    part = jnp.dot((hid * hid).astype(BF16), w2_ref[...], preferred_element_type=F32)
    o_ref[...] += mod_ref[5:6, :] * part


def _mlp(x2, mod, gain, w1, w2, seq):
    n, d = x2.shape
    tm, tf = MLP_TM, MLP_TF
    per_batch = seq // tm
    return pl.pallas_call(
        _mlp_kernel,
        grid=(n // tm, D_FF // tf),
        in_specs=[
            pl.BlockSpec((tm, d), lambda i, j: (i, 0)),
            pl.BlockSpec((None, 6, d), lambda i, j: (i // per_batch, 0, 0)),
            pl.BlockSpec((1, d), lambda i, j: (0, 0)),
            pl.BlockSpec((None, d, tf), lambda i, j: (0, 0, j)),
            pl.BlockSpec((None, tf, d), lambda i, j: (0, j, 0)),
        ],
        out_specs=pl.BlockSpec((tm, d), lambda i, j: (i, 0)),
        out_shape=jax.ShapeDtypeStruct((n, d), F32),
        scratch_shapes=[pltpu.VMEM((tm, d), BF16)],
        compiler_params=_params("arbitrary", "arbitrary"),
        name="mlp",
    )(x2, mod, gain, w1, w2)


def kernel(x, c, w_ada, b_ada, norm_gain, w_in, b_f, b_gate, q_gain, k_gain, conf_dw, conf_db,
           conf_ln_g, conf_ln_b, pool_w, pool_scale, sconv_w, w_branch, w_out, w_mlp1, w_mlp2):
    batch, seq, d = x.shape
    depth = w_ada.shape[0]
    W = WIDTH
    att_cols = 3 * W + HEADS

    w_t = jnp.swapaxes(w_in, 1, 2)
    w_packed = _repack_w_in(w_t, 0)
    w_forget = jnp.pad(w_t[:, 3 * W:att_cols], ((0, 0), (0, LANES - HEADS), (0, 0))).astype(BF16)
    pool_w_b = pool_w.astype(BF16)
    b_gate3 = b_gate.reshape(depth, 1, 4 * d)

    mod_all = _ada(c, w_ada, b_ada).reshape(depth, batch, 6, d)
    x2 = x.reshape(batch * seq, d)
    for l in range(depth):
        mod = mod_all[l]
        b_f_row = jnp.pad(b_f[l], (0, LANES - HEADS)).reshape(1, LANES)

        z, zf, h = _inproj(x2, mod, norm_gain[l, 0:1], w_packed, w_forget, l, seq)
        y_att, *side = _attention(z, zf, b_f_row,
                                  jnp.tile(q_gain[l], HEADS_PER_STEP).reshape(1, LANES),
                                  jnp.tile(k_gain[l], HEADS_PER_STEP).reshape(1, LANES), batch, seq,
                                  cast=(w_mlp1, w_mlp2, w_out, w_branch.reshape(depth, 4 * W, d)),
                                  cast_layer=l, repack=(w_t, l + 1) if l + 1 < depth else None)
        w_packed_next = side.pop() if l + 1 < depth else None
        w_mlp1_b, w_mlp2_b, w_out_b, w_branch_b = side
        w_branch_b = w_branch_b.reshape(1, 4, W, d)
        y_rest = _conv_mixers(z, conf_dw[l], conf_db[l].reshape(1, W), conf_ln_g[l].reshape(1, W),
                              conf_ln_b[l].reshape(1, W), pool_w_b, l,
                              pool_scale[l].reshape(1, W), sconv_w[l], batch, seq)
        merged = _gated_merge(h, y_att, y_rest, w_packed, b_gate3, w_branch_b, l)
        x2 = _outproj(merged, x2, mod, w_out_b, seq)
        x2 = _mlp(x2, mod, norm_gain[l, 1:2], w_mlp1_b, w_mlp2_b, seq)
        w_packed = w_packed_next
    return x2.reshape(batch, seq, d)
```

```python
import functools
import math

import jax
import jax.numpy as jnp
from jax import lax
from jax.experimental import pallas as pl
from jax.experimental.pallas import tpu as pltpu

F32 = jnp.float32
BF16 = jnp.bfloat16

D_MODEL = 2048
WIDTH = D_MODEL // 4
HEADS = 8
HEAD_DIM = WIDTH // HEADS
CONF_K = 31
POOL_WINDOWS = (2, 4, 8, 16)
POOL_GROUP = WIDTH // len(POOL_WINDOWS)
SHORT_K = 3
D_FF = 4 * D_MODEL
RMS_EPS = 1e-6
LN_EPS = 1e-5

LANES = 128
SUBLANES = 8
VMEM_LIMIT = 60 * 1024 * 1024

ADA_TK = 256
ADA_SPLIT = 4
INPROJ_TM = 1024
INPROJ_TN = 1536
ATT_BQ = 256
HEADS_PER_STEP = LANES // HEAD_DIM
ATT_AUG = 3
LOG2E = math.log2(math.e)
CONV_TT = 512
CONV_HALO = 32
CONV_ROWS = 32
POOL_HIST = 16
GATE_TM = 2048
GATE_TN = 256
OUTPROJ_TM = 1024
MLP_TM = 1024
MLP_TF = 512

MIX_COLS = 9 * WIDTH
GATE_COLS = 4 * D_MODEL
ATT_COLS = 3 * WIDTH
REPACK_ROWS = 512


def _params(*sem):
    return pltpu.CompilerParams(dimension_semantics=sem, vmem_limit_bytes=VMEM_LIMIT)


def _sigmoid(x):
    return 1.0 / (1.0 + jnp.exp(-x))


def _modulated_rmsnorm(x, gain, shift, scale):
    y = x * lax.rsqrt(jnp.mean(x * x, axis=-1, keepdims=True) + RMS_EPS) * gain
    return y * (1.0 + scale) + shift


_NT = (((1,), (1,)), ((), ()))


def _repacked_rows(main, tail, first_row):
    shifted = jnp.concatenate([main[HEADS:, :], tail], axis=0)
    row = first_row + lax.broadcasted_iota(jnp.int32, (main.shape[0], 1), 0)
    return jnp.where(row >= ATT_COLS, shifted, main).astype(BF16)


def _repack_specs(layer, rows, first, index):
    main = pl.BlockSpec((None, rows, D_MODEL), lambda *g: (layer, first + index(*g), 0))
    tail = pl.BlockSpec((None, HEADS, D_MODEL),
                        lambda *g: (layer, (first + index(*g) + 1) * (rows // HEADS), 0))
    out = pl.BlockSpec((None, rows, D_MODEL), lambda *g: (0, index(*g), 0))
    return [main, tail], out


def _repack_kernel(w_ref, tail_ref, o_ref):
    o_ref[...] = _repacked_rows(w_ref[...], tail_ref[...], pl.program_id(0) * w_ref.shape[0])


def _repack_mixer_rows(w_t, layer):
    in_specs, out_spec = _repack_specs(layer, REPACK_ROWS, 0, lambda r: r)
    return pl.pallas_call(
        _repack_kernel,
        grid=(MIX_COLS // REPACK_ROWS,),
        in_specs=in_specs,
        out_specs=out_spec,
        out_shape=jax.ShapeDtypeStruct((1, MIX_COLS, w_t.shape[2]), BF16),
        compiler_params=_params("arbitrary"),
        name="repack",
    )(w_t, w_t)


def _ada_kernel(c_ref, *refs):
    w_refs, b_ref, o_ref = refs[:ADA_SPLIT], refs[ADA_SPLIT], refs[ADA_SPLIT + 1]

    @pl.when(pl.program_id(1) == 0)
    def _():
        o_ref[...] = jnp.broadcast_to(b_ref[...], o_ref.shape)

    cb = c_ref[...].astype(BF16)
    part = o_ref.shape[1] // ADA_SPLIT
    for s, w_ref in enumerate(w_refs):
        o_ref[:, s * part:(s + 1) * part] += jnp.dot(cb, w_ref[...].astype(BF16),
                                                     preferred_element_type=F32)


def _ada(c, w_ada, b_ada):
    depth, d, n = w_ada.shape
    b = c.shape[0]
    part = n // ADA_SPLIT
    w_spec = lambda s: pl.BlockSpec((None, ADA_TK, part), lambda l, k, s=s: (l, k, s))
    return pl.pallas_call(
        _ada_kernel,
        grid=(depth, d // ADA_TK),
        in_specs=[pl.BlockSpec((b, ADA_TK), lambda l, k: (0, k))]
        + [w_spec(s) for s in range(ADA_SPLIT)]
        + [pl.BlockSpec((None, 1, n), lambda l, k: (l, 0, 0))],
        out_specs=pl.BlockSpec((None, b, n), lambda l, k: (l, 0, 0)),
        out_shape=jax.ShapeDtypeStruct((depth, b, n), F32),
        compiler_params=_params("arbitrary", "arbitrary"),
        name="ada",
    )(c, *([w_ada] * ADA_SPLIT), b_ada.reshape(depth, 1, n))


def _inproj_kernel(x_ref, mod_ref, gain_ref, w_ref, wf_ref, z_ref, zf_ref, h_ref):
    @pl.when(pl.program_id(1) == 0)
    def _():
        h = _modulated_rmsnorm(x_ref[...], gain_ref[...], mod_ref[0:1, :], mod_ref[1:2, :])
        hb = h.astype(BF16)
        h_ref[...] = hb
        zf_ref[...] = lax.dot_general(hb, wf_ref[...], _NT, preferred_element_type=F32)

    z_ref[...] = lax.dot_general(h_ref[...], w_ref[...], _NT,
                                 preferred_element_type=F32).astype(z_ref.dtype)


def _inproj(x2, mod, gain, w_mix, w_forget, layer, seq):
    n, d = x2.shape
    tm, tn = INPROJ_TM, INPROJ_TN
    per_batch = seq // tm
    return pl.pallas_call(
        _inproj_kernel,
        grid=(n // tm, MIX_COLS // tn),
        in_specs=[
            pl.BlockSpec((tm, d), lambda i, j: (i, 0)),
            pl.BlockSpec((None, 6, d), lambda i, j: (i // per_batch, 0, 0)),
            pl.BlockSpec((1, d), lambda i, j: (0, 0)),
            pl.BlockSpec((None, tn, d), lambda i, j: (0, j, 0)),
            pl.BlockSpec((None, LANES, d), lambda i, j: (layer, 0, 0)),
        ],
        out_specs=[
            pl.BlockSpec((tm, tn), lambda i, j: (i, j)),
            pl.BlockSpec((tm, LANES), lambda i, j: (i, 0)),
            pl.BlockSpec((tm, d), lambda i, j: (i, 0)),
        ],
        out_shape=[
            jax.ShapeDtypeStruct((n, MIX_COLS), BF16),
            jax.ShapeDtypeStruct((n, LANES), F32),
            jax.ShapeDtypeStruct((n, d), BF16),
        ],
        compiler_params=_params("arbitrary", "arbitrary"),
        name="inproj",
    )(x2, mod, gain, w_mix, w_forget)


def _attn_kernel(q_ref, k_ref, v_ref, zf_ref, bf_ref, qg_ref, kg_ref, *rest, n_cast, repack_first):
    n_repack = len(repack_first)
    n_in = n_cast + 2 * n_repack
    side_in, rest = rest[:n_in], rest[n_in:]
    o_ref, side_out = rest[0], rest[1:1 + n_cast + n_repack]
    cum_ref, qa_ref, ka_ref, s_ref, p_ref = rest[1 + n_cast + n_repack:]
    for src, dst in zip(side_in[:n_cast], side_out[:n_cast]):
        dst[...] = src[...].astype(BF16)
    step = pl.program_id(0) * pl.num_programs(1) + pl.program_id(1)
    for r, first in enumerate(repack_first):
        main_ref, tail_ref = side_in[n_cast + 2 * r], side_in[n_cast + 2 * r + 1]
        side_out[n_cast + r][...] = _repacked_rows(main_ref[...], tail_ref[...],
                                                   (first + step) * main_ref.shape[0])

    seq = q_ref.shape[0]
    g = pl.program_id(1)

    @pl.when(g == 0)
    def _():
        logit = zf_ref[...] + bf_ref[...]
        log_f = jnp.minimum(logit, 0.0) - jnp.log1p(jnp.exp(-jnp.abs(logit)))
        acc = log_f.T[0:HEADS, :]
        lane = lax.broadcasted_iota(jnp.int32, acc.shape, 1)
        shift = 1
        while shift < seq:
            acc = acc + jnp.where(lane >= shift, pltpu.roll(acc, shift, axis=1), 0.0)
            shift *= 2
        pad = jnp.zeros((LANES - HEADS, seq), F32)
        rest = jnp.concatenate([acc * LOG2E, pad], axis=0).T
        for j in range(ATT_AUG):
            piece = rest.astype(BF16)
            cum_ref[:, j * LANES:(j + 1) * LANES] = piece
            rest = rest - piece.astype(F32)

    scale = 1.0 / math.sqrt(HEAD_DIM)
    lane = lax.broadcasted_iota(jnp.int32, (seq, LANES), 1)
    in_h0 = lane < HEAD_DIM

    def normalise(ref, gain_ref, mult):
        x = ref[...].astype(F32)
        sq = x * x
        ms0 = jnp.sum(jnp.where(in_h0, sq, 0.0), axis=1, keepdims=True)
        ms1 = jnp.sum(jnp.where(in_h0, 0.0, sq), axis=1, keepdims=True)
        inv = lax.rsqrt(jnp.where(in_h0, ms0, ms1) * (1.0 / HEAD_DIM) + RMS_EPS)
        return x * inv * (gain_ref[...] * mult)

    qn = normalise(q_ref, qg_ref, scale * LOG2E)
    kn = normalise(k_ref, kg_ref, 1.0)

    e_row = lax.broadcasted_iota(jnp.int32, (LANES, LANES), 0)
    e_col = lax.broadcasted_iota(jnp.int32, (LANES, LANES), 1)
    for hh in range(HEADS_PER_STEP):
        head = g * HEADS_PER_STEP + hh
        base = HEAD_DIM * (1 - hh)
        pick = e_row == head
        sel = []
        for j in range(ATT_AUG):
            plus = jnp.where(pick & (e_col == base + j), 1.0, 0.0)
            minus = jnp.where(pick & (e_col == base + ATT_AUG + j), 1.0, 0.0)
            sel.append(plus - minus)
        sel = jnp.concatenate(sel, axis=0).astype(BF16)
        aug = jnp.dot(cum_ref[...], sel, preferred_element_type=F32)
        in_head = in_h0 if hh == 0 else jnp.logical_not(in_h0)
        first = (lane >= base) & (lane < base + ATT_AUG)
        second = (lane >= base + ATT_AUG) & (lane < base + 2 * ATT_AUG)
        qa = jnp.where(in_head, qn, jnp.where(first, aug, jnp.where(second, 1.0, 0.0)))
        ka = jnp.where(in_head, kn, jnp.where(first, 1.0, jnp.where(second, aug, 0.0)))
        qa_ref[hh] = qa.astype(BF16)
        ka_ref[hh] = ka.astype(BF16)

    bq = ATT_BQ
    row = lax.broadcasted_iota(jnp.int32, (bq, bq), 0)
    col = lax.broadcasted_iota(jnp.int32, (bq, bq), 1)
    causal = row >= col
    out_h0 = lax.broadcasted_iota(jnp.int32, (bq, LANES), 1) < HEAD_DIM
    nt = (((1,), (1,)), ((), ()))
    def scores(i, hh, slot):
        qa = qa_ref[hh, i * bq:(i + 1) * bq, :]
        m_part = None
        for c in range(i + 1):
            k0 = c * bq
            s = lax.dot_general(qa, ka_ref[hh, k0:k0 + bq, :], nt, preferred_element_type=F32)
            if c == i:
                s = jnp.where(causal, s, -jnp.inf)
            s_ref[slot, :, k0:k0 + bq] = s
            for h0 in range(0, bq, LANES):
                part = s[:, h0:h0 + LANES]
                m_part = part if m_part is None else jnp.maximum(m_part, part)
        return jnp.max(m_part, axis=1, keepdims=True)

    def weighted_values(i, slot, m):
        l_part = None
        for c in range(i + 1):
            k0 = c * bq
            p = jnp.exp2(s_ref[slot, :, k0:k0 + bq] - m)
            for h0 in range(0, bq, LANES):
                part = p[:, h0:h0 + LANES]
                l_part = part if l_part is None else l_part + part
            p_ref[slot, :, k0:k0 + bq] = p.astype(BF16)
        acc = jnp.dot(p_ref[slot, :, 0:(i + 1) * bq], v_ref[0:(i + 1) * bq, :],
                      preferred_element_type=F32)
        return acc / jnp.sum(l_part, axis=1, keepdims=True)

    units = [(i, hh) for i in range(seq // bq) for hh in range(HEADS_PER_STEP)]
    m_next = scores(*units[0], 0)
    outs = []
    for u, (i, hh) in enumerate(units):
        m_cur = m_next
        if u + 1 < len(units):
            m_next = scores(*units[u + 1], (u + 1) % 2)
        outs.append(weighted_values(i, u % 2, m_cur))
        if hh == HEADS_PER_STEP - 1:
            o_ref[i * bq:(i + 1) * bq, :] = jnp.where(out_h0, outs[0], outs[1]).astype(o_ref.dtype)
            outs = []


def _attention(z, zf, b_f_row, q_gain, k_gain, batch, seq, cast=(), cast_layer=0, repack=()):
    n = z.shape[0]
    groups = HEADS // HEADS_PER_STEP
    steps = batch * groups
    blk = lambda off: pl.BlockSpec((seq, LANES), lambda b, g, off=off: (b, off + g))
    slab = lambda w, layer: pl.BlockSpec((1, w.shape[1] // steps, w.shape[2]),
                                         lambda b, g: (layer, b * groups + g, 0))
    side_in, side_specs_in = list(cast), [slab(w, cast_layer) for w in cast]
    side_specs_out = [slab(w, 0) for w in cast]
    side_shapes = [jax.ShapeDtypeStruct((1,) + w.shape[1:], BF16) for w in cast]
    repack_first = []
    for w_t, layer, first_row, n_rows in repack:
        rows = n_rows // steps
        repack_first.append(first_row // rows)
        specs_in, spec_out = _repack_specs(layer, rows, first_row // rows,
                                           lambda b, g: b * groups + g)
        side_in += [w_t, w_t]
        side_specs_in += specs_in
        side_specs_out.append(spec_out)
        side_shapes.append(jax.ShapeDtypeStruct((1, n_rows, w_t.shape[2]), BF16))
    outs = pl.pallas_call(
        functools.partial(_attn_kernel, n_cast=len(cast), repack_first=tuple(repack_first)),
        grid=(batch, groups),
        in_specs=[
            blk(0), blk(groups), blk(2 * groups),
            pl.BlockSpec((seq, LANES), lambda b, g: (b, 0)),
            pl.BlockSpec((1, LANES), lambda b, g: (0, 0)),
            pl.BlockSpec((1, LANES), lambda b, g: (0, 0)),
            pl.BlockSpec((1, LANES), lambda b, g: (0, 0)),
        ] + side_specs_in,
        out_specs=[pl.BlockSpec((seq, LANES), lambda b, g: (b, g))] + side_specs_out,
        out_shape=[jax.ShapeDtypeStruct((n, WIDTH), BF16)] + side_shapes,
        scratch_shapes=[
            pltpu.VMEM((seq, ATT_AUG * LANES), BF16),
            pltpu.VMEM((HEADS_PER_STEP, seq, LANES), BF16),
            pltpu.VMEM((HEADS_PER_STEP, seq, LANES), BF16),
            pltpu.VMEM((2, ATT_BQ, seq), F32),
            pltpu.VMEM((2, ATT_BQ, seq), BF16),
        ],
        compiler_params=_params("arbitrary", "arbitrary"),
        name="attn",
    )(z, z, z, zf, b_f_row, q_gain, k_gain, *side_in)
    return outs


def _conv_kernel(a_ref, g_ref, p_ref, x_ref, b_ref, c_ref,
                 ah_ref, gh_ref, ph_ref, xh_ref, ch_ref,
                 dw_ref, db_ref, lng_ref, lnb_ref, pw_ref, ps_ref, sw_ref,
                 o_ref, ubuf, pbuf, vbuf):
    tt = a_ref.shape[0]
    t = pl.program_id(1)
    has_past = t > 0
    H = CONV_HALO

    u_h = ah_ref[...].astype(F32) * _sigmoid(gh_ref[...].astype(F32))
    ubuf[0:H, :] = jnp.where(has_past, u_h, 0.0)
    ubuf[H:, :] = a_ref[...].astype(F32) * _sigmoid(g_ref[...].astype(F32))
    pbuf[0:H, :] = jnp.where(has_past, ph_ref[...].astype(F32), 0.0)
    pbuf[H:, :] = p_ref[...].astype(F32)
    v_h = ch_ref[...].astype(F32) * xh_ref[...].astype(F32)
    vbuf[0:H, :] = jnp.where(has_past, v_h, 0.0)
    vbuf[H:, :] = c_ref[...].astype(F32) * x_ref[...].astype(F32)

    R = CONV_ROWS

    def chunk(ci, carry):
        r0 = pl.multiple_of(ci * R, R)

        acc = jnp.broadcast_to(db_ref[...], (R, WIDTH))
        win = ubuf[pl.ds(r0, R + H), :]
        for r in range(SUBLANES):
            rolled = win if r == 0 else pltpu.roll(win, r, axis=0)
            for a in range((CONF_K - 1 - r) // SUBLANES + 1):
                k = CONF_K - 1 - (SUBLANES * a + r)
                lo = H - SUBLANES * a
                acc = acc + dw_ref[k:k + 1, :] * rolled[lo:lo + R, :]
        mu = jnp.mean(acc, axis=-1, keepdims=True)
        xc = acc - mu
        y = xc * lax.rsqrt(jnp.mean(xc * xc, axis=-1, keepdims=True) + LN_EPS)
        y = y * lng_ref[...] + lnb_ref[...]
        o_ref[pl.ds(r0, R), 0:WIDTH] = (y * _sigmoid(y)).astype(o_ref.dtype)

        pos = t * tt + r0 + lax.broadcasted_iota(jnp.int32, (R, 1), 0)
        pooled = []
        for gi, w in enumerate(POOL_WINDOWS):
            lo = gi * POOL_GROUP
            tot = pbuf[pl.ds(r0 + (H - POOL_HIST), R + POOL_HIST), lo:lo + POOL_GROUP]
            tok = tot[POOL_HIST:, :]
            span = 1
            while span < w:
                tot = tot + pltpu.roll(tot, span, axis=0)
                span *= 2
            cnt = jnp.minimum(pos + 1, w).astype(F32)
            dlt = (tot[POOL_HIST:, :] / cnt - tok).astype(BF16)
            pooled.append(jnp.dot(dlt, pw_ref[gi], preferred_element_type=F32))
        yp = jnp.concatenate(pooled, axis=1) * ps_ref[...]
        o_ref[pl.ds(r0, R), WIDTH:2 * WIDTH] = yp.astype(o_ref.dtype)

        vwin = vbuf[pl.ds(r0 + (H - SUBLANES), R + SUBLANES), :]
        sc = sw_ref[SHORT_K - 1:SHORT_K, :] * vwin[SUBLANES:, :]
        for dly in range(1, SHORT_K):
            k = SHORT_K - 1 - dly
            sc = sc + sw_ref[k:k + 1, :] * pltpu.roll(vwin, dly, axis=0)[SUBLANES:, :]
        ys = b_ref[pl.ds(r0, R), :].astype(F32) * sc
        o_ref[pl.ds(r0, R), 2 * WIDTH:3 * WIDTH] = ys.astype(o_ref.dtype)
        return carry

    lax.fori_loop(0, tt // R, chunk, 0, unroll=2)


def _conv_mixers(z, conf_dw, conf_db, ln_g, ln_b, pool_w, layer, pool_scale, sconv_w, batch, seq):
    n = z.shape[0]
    tt, H = CONV_TT, CONV_HALO
    tiles = seq // tt
    main = lambda cb: pl.BlockSpec((tt, WIDTH), lambda b, t, cb=cb: (b * tiles + t, cb))
    halo = lambda cb: pl.BlockSpec(
        (H, WIDTH),
        lambda b, t, cb=cb: (jnp.maximum((b * tiles + t) * (tt // H) - 1, 0), cb))
    full = lambda shape: pl.BlockSpec(shape, lambda b, t: (0,) * len(shape))
    return pl.pallas_call(
        _conv_kernel,
        grid=(batch, tiles),
        in_specs=[
            main(3), main(4), main(5), main(6), main(7), main(8),
            halo(3), halo(4), halo(5), halo(6), halo(8),
            full((CONF_K, WIDTH)), full((1, WIDTH)), full((1, WIDTH)), full((1, WIDTH)),
            pl.BlockSpec((None, len(POOL_WINDOWS), POOL_GROUP, POOL_GROUP),
                         lambda b, t: (layer, 0, 0, 0)),
            full((1, WIDTH)),
            full((SHORT_K, WIDTH)),
        ],
        out_specs=pl.BlockSpec((tt, 3 * WIDTH), lambda b, t: (b * tiles + t, 0)),
        out_shape=jax.ShapeDtypeStruct((n, 3 * WIDTH), BF16),
        scratch_shapes=[pltpu.VMEM((H + tt, WIDTH), F32)] * 3,
        compiler_params=_params("arbitrary", "arbitrary"),
        name="conv",
    )(z, z, z, z, z, z, z, z, z, z, z,
      conf_dw, conf_db, ln_g, ln_b, pool_w, pool_scale, sconv_w)


def _gate_kernel(h_ref, ya_ref, yr_ref,
                 wg0_ref, wg1_ref, wg2_ref, wg3_ref, bg0_ref, bg1_ref, bg2_ref, bg3_ref,
                 wbr_ref, o_ref):
    h = h_ref[...]
    wgs = (wg0_ref, wg1_ref, wg2_ref, wg3_ref)
    bgs = (bg0_ref, bg1_ref, bg2_ref, bg3_ref)
    merged = None
    for b in range(4):
        y = ya_ref[...] if b == 0 else yr_ref[:, (b - 1) * WIDTH:b * WIDTH]
        logit = lax.dot_general(h, wgs[b][...], _NT, preferred_element_type=F32)
        gate = _sigmoid(logit + bgs[b][...])
        term = gate * jnp.dot(y, wbr_ref[b], preferred_element_type=F32)
        merged = term if merged is None else merged + term
    o_ref[...] = merged.astype(BF16)


def _gated_merge(h, y_att, y_rest, w_gate, b_gate, w_branch, layer):
    n, d = h.shape
    tm, tn = GATE_TM, GATE_TN
    na = d // tn
    wg = lambda b: pl.BlockSpec((None, tn, d), lambda i, j, b=b: (0, b * na + j, 0))
    bg = lambda b: pl.BlockSpec((None, 1, tn), lambda i, j, b=b: (layer, 0, b * na + j))
    return pl.pallas_call(
        _gate_kernel,
        grid=(n // tm, na),
        in_specs=[
            pl.BlockSpec((tm, d), lambda i, j: (i, 0)),
            pl.BlockSpec((tm, WIDTH), lambda i, j: (i, 0)),
            pl.BlockSpec((tm, 3 * WIDTH), lambda i, j: (i, 0)),
            wg(0), wg(1), wg(2), wg(3), bg(0), bg(1), bg(2), bg(3),
            pl.BlockSpec((None, 4, WIDTH, tn), lambda i, j: (0, 0, 0, j)),
        ],
        out_specs=pl.BlockSpec((tm, tn), lambda i, j: (i, j)),
        out_shape=jax.ShapeDtypeStruct((n, d), BF16),
        compiler_params=_params("arbitrary", "arbitrary"),
        name="gate",
    )(h, y_att, y_rest, w_gate, w_gate, w_gate, w_gate,
      b_gate, b_gate, b_gate, b_gate, w_branch)


def _outproj_kernel(m_ref, x_ref, mod_ref, wout_ref, o_ref):
    proj = jnp.dot(m_ref[...], wout_ref[...], preferred_element_type=F32)
    o_ref[...] = x_ref[...] + mod_ref[2:3, :] * proj


def _outproj(merged, x2, mod, w_out, seq):
    n, d = x2.shape
    tm = OUTPROJ_TM
    per_batch = seq // tm
    return pl.pallas_call(
        _outproj_kernel,
        grid=(n // tm,),
        in_specs=[
            pl.BlockSpec((tm, d), lambda i: (i, 0)),
            pl.BlockSpec((tm, d), lambda i: (i, 0)),
            pl.BlockSpec((None, 6, d), lambda i: (i // per_batch, 0, 0)),
            pl.BlockSpec((None, d, d), lambda i: (0, 0, 0), pipeline_mode=pl.Buffered(1)),
        ],
        out_specs=pl.BlockSpec((tm, d), lambda i: (i, 0)),
        out_shape=jax.ShapeDtypeStruct((n, d), F32),
        compiler_params=_params("arbitrary"),
        name="outproj",
    )(merged, x2, mod, w_out)


def _mlp_kernel(x_ref, mod_ref, gain_ref, w1_ref, w2_ref, o_ref, h_ref):
    j = pl.program_id(1)

    @pl.when(j == 0)
    def _():
        x = x_ref[...]
        h = _modulated_rmsnorm(x, gain_ref[...], mod_ref[3:4, :], mod_ref[4:5, :])
        h_ref[...] = h.astype(BF16)
        o_ref[...] = x

    hid = jnp.maximum(jnp.dot(h_ref[...], w1_ref[...], preferred_element_type=F32), 0.0)
    part = jnp.dot((hid * hid).astype(BF16), w2_ref[...], preferred_element_type=F32)
    o_ref[...] += mod_ref[5:6, :] * part


def _mlp(x2, mod, gain, w1, w2, seq):
    n, d = x2.shape
    tm, tf = MLP_TM, MLP_TF
    per_batch = seq // tm
    return pl.pallas_call(
        _mlp_kernel,
        grid=(n // tm, D_FF // tf),
        in_specs=[
            pl.BlockSpec((tm, d), lambda i, j: (i, 0)),
            pl.BlockSpec((None, 6, d), lambda i, j: (i // per_batch, 0, 0)),
            pl.BlockSpec((1, d), lambda i, j: (0, 0)),
            pl.BlockSpec((None, d, tf), lambda i, j: (0, 0, j)),
            pl.BlockSpec((None, tf, d), lambda i, j: (0, j, 0)),
        ],
        out_specs=pl.BlockSpec((tm, d), lambda i, j: (i, 0)),
        out_shape=jax.ShapeDtypeStruct((n, d), F32),
        scratch_shapes=[pltpu.VMEM((tm, d), BF16)],
        compiler_params=_params("arbitrary", "arbitrary"),
        name="mlp",
    )(x2, mod, gain, w1, w2)


def kernel(x, c, w_ada, b_ada, norm_gain, w_in, b_f, b_gate, q_gain, k_gain, conf_dw, conf_db,
           conf_ln_g, conf_ln_b, pool_w, pool_scale, sconv_w, w_branch, w_out, w_mlp1, w_mlp2):
    batch, seq, d = x.shape
    depth = w_ada.shape[0]
    W = WIDTH
    att_cols = 3 * W + HEADS

    w_t = jnp.swapaxes(w_in, 1, 2)
    w_mix = _repack_mixer_rows(w_t, 0)
    w_forget = jnp.pad(w_t[:, 3 * W:att_cols], ((0, 0), (0, LANES - HEADS), (0, 0))).astype(BF16)
    pool_w_b = pool_w.astype(BF16)
    b_gate3 = b_gate.reshape(depth, 1, 4 * d)

    mod_all = _ada(c, w_ada, b_ada).reshape(depth, batch, 6, d)
    x2 = x.reshape(batch * seq, d)
    for l in range(depth):
        mod = mod_all[l]
        b_f_row = jnp.pad(b_f[l], (0, LANES - HEADS)).reshape(1, LANES)

        z, zf, h = _inproj(x2, mod, norm_gain[l, 0:1], w_mix, w_forget, l, seq)
        y_rest = _conv_mixers(z, conf_dw[l], conf_db[l].reshape(1, W), conf_ln_g[l].reshape(1, W),
                              conf_ln_b[l].reshape(1, W), pool_w_b, l,
                              pool_scale[l].reshape(1, W), sconv_w[l], batch, seq)
        repack = [(w_t, l, MIX_COLS, GATE_COLS)]
        if l + 1 < depth:
            repack.append((w_t, l + 1, 0, MIX_COLS))
        y_att, *side = _attention(z, zf, b_f_row,
                                  jnp.tile(q_gain[l], HEADS_PER_STEP).reshape(1, LANES),
                                  jnp.tile(k_gain[l], HEADS_PER_STEP).reshape(1, LANES), batch, seq,
                                  cast=(w_mlp1, w_mlp2, w_out, w_branch.reshape(depth, 4 * W, d)),
                                  cast_layer=l, repack=repack)
        w_mlp1_b, w_mlp2_b, w_out_b, w_branch_b, w_gate = side[:5]
        w_mix = side[5] if l + 1 < depth else None
        w_branch_b = w_branch_b.reshape(1, 4, W, d)
        merged = _gated_merge(h, y_att, y_rest, w_gate, b_gate3, w_branch_b, l)
        x2 = _outproj(merged, x2, mod, w_out_b, seq)
        x2 = _mlp(x2, mod, norm_gain[l, 1:2], w_mlp1_b, w_mlp2_b, seq)
    return x2.reshape(batch, seq, d)
```

```python
import functools
import math

import jax
import jax.numpy as jnp
from jax import lax
from jax.experimental import pallas as pl
from jax.experimental.pallas import tpu as pltpu

F32 = jnp.float32
BF16 = jnp.bfloat16

D_MODEL = 2048
WIDTH = D_MODEL // 4
HEADS = 8
HEAD_DIM = WIDTH // HEADS
CONF_K = 31
POOL_WINDOWS = (2, 4, 8, 16)
POOL_GROUP = WIDTH // len(POOL_WINDOWS)
SHORT_K = 3
D_FF = 4 * D_MODEL
RMS_EPS = 1e-6
LN_EPS = 1e-5

LANES = 128
SUBLANES = 8
VMEM_LIMIT = 60 * 1024 * 1024

ADA_TK = 256
ADA_SPLIT = 4
INPROJ_TM = 1024
INPROJ_TN = 1536
ATT_BQ = 256
HEADS_PER_STEP = LANES // HEAD_DIM
ATT_AUG = 3
LOG2E = math.log2(math.e)
CONV_TT = 512
CONV_HALO = 32
CONV_ROWS = 32
POOL_HIST = 16
GATE_TM = 2048
GATE_TN = 256
OUTPROJ_TM = 1024
MLP_TM = 1024
MLP_TF = 512

MIX_COLS = 9 * WIDTH
GATE_COLS = 4 * D_MODEL
ATT_COLS = 3 * WIDTH
REPACK_ROWS = 512


def _params(*sem):
    return pltpu.CompilerParams(dimension_semantics=sem, vmem_limit_bytes=VMEM_LIMIT)


def _sigmoid(x):
    return 1.0 / (1.0 + jnp.exp(-x))


def _modulated_rmsnorm(x, gain, shift, scale):
    inv = lax.rsqrt(jnp.mean(x * x, axis=-1, keepdims=True) + RMS_EPS)
    return x * inv * (gain * (1.0 + scale)) + shift


_NT = (((1,), (1,)), ((), ()))


def _repacked_rows(main, tail, first_row):
    shifted = jnp.concatenate([main[HEADS:, :], tail], axis=0)
    row = first_row + lax.broadcasted_iota(jnp.int32, (main.shape[0], 1), 0)
    return jnp.where(row >= ATT_COLS, shifted, main).astype(BF16)


def _repack_specs(layer, rows, first, index):
    main = pl.BlockSpec((None, rows, D_MODEL), lambda *g: (layer, first + index(*g), 0))
    tail = pl.BlockSpec((None, HEADS, D_MODEL),
                        lambda *g: (layer, (first + index(*g) + 1) * (rows // HEADS), 0))
    out = pl.BlockSpec((None, rows, D_MODEL), lambda *g: (0, index(*g), 0))
    return [main, tail], out


def _repack_kernel(w_ref, tail_ref, o_ref):
    o_ref[...] = _repacked_rows(w_ref[...], tail_ref[...], pl.program_id(0) * w_ref.shape[0])


def _repack_mixer_rows(w_t, layer):
    in_specs, out_spec = _repack_specs(layer, REPACK_ROWS, 0, lambda r: r)
    return pl.pallas_call(
        _repack_kernel,
        grid=(MIX_COLS // REPACK_ROWS,),
        in_specs=in_specs,
        out_specs=out_spec,
        out_shape=jax.ShapeDtypeStruct((1, MIX_COLS, w_t.shape[2]), BF16),
        compiler_params=_params("arbitrary"),
        name="repack",
    )(w_t, w_t)


def _ada_kernel(c_ref, *refs):
    w_refs, b_ref, o_ref = refs[:ADA_SPLIT], refs[ADA_SPLIT], refs[ADA_SPLIT + 1]

    @pl.when(pl.program_id(1) == 0)
    def _():
        o_ref[...] = jnp.broadcast_to(b_ref[...], o_ref.shape)

    cb = c_ref[...].astype(BF16)
    part = o_ref.shape[1] // ADA_SPLIT
    for s, w_ref in enumerate(w_refs):
        o_ref[:, s * part:(s + 1) * part] += jnp.dot(cb, w_ref[...].astype(BF16),
                                                     preferred_element_type=F32)


def _ada(c, w_ada, b_ada):
    depth, d, n = w_ada.shape
    b = c.shape[0]
    part = n // ADA_SPLIT
    w_spec = lambda s: pl.BlockSpec((None, ADA_TK, part), lambda l, k, s=s: (l, k, s))
    return pl.pallas_call(
        _ada_kernel,
        grid=(depth, d // ADA_TK),
        in_specs=[pl.BlockSpec((b, ADA_TK), lambda l, k: (0, k))]
        + [w_spec(s) for s in range(ADA_SPLIT)]
        + [pl.BlockSpec((None, 1, n), lambda l, k: (l, 0, 0))],
        out_specs=pl.BlockSpec((None, b, n), lambda l, k: (l, 0, 0)),
        out_shape=jax.ShapeDtypeStruct((depth, b, n), F32),
        compiler_params=_params("arbitrary", "arbitrary"),
        name="ada",
    )(c, *([w_ada] * ADA_SPLIT), b_ada.reshape(depth, 1, n))


def _inproj_kernel(x_ref, mod_ref, gain_ref, w_ref, wf_ref, z_ref, zf_ref, h_ref):
    @pl.when(pl.program_id(1) == 0)
    def _():
        h = _modulated_rmsnorm(x_ref[...], gain_ref[...], mod_ref[0:1, :], mod_ref[1:2, :])
        hb = h.astype(BF16)
        h_ref[...] = hb
        zf_ref[...] = lax.dot_general(hb, wf_ref[...], _NT, preferred_element_type=F32)

    z_ref[...] = lax.dot_general(h_ref[...], w_ref[...], _NT,
                                 preferred_element_type=F32).astype(z_ref.dtype)


def _inproj(x2, mod, gain, w_mix, w_forget, layer, seq):
    n, d = x2.shape
    tm, tn = INPROJ_TM, INPROJ_TN
    per_batch = seq // tm
    return pl.pallas_call(
        _inproj_kernel,
        grid=(n // tm, MIX_COLS // tn),
        in_specs=[
            pl.BlockSpec((tm, d), lambda i, j: (i, 0)),
            pl.BlockSpec((None, 6, d), lambda i, j: (i // per_batch, 0, 0)),
            pl.BlockSpec((1, d), lambda i, j: (0, 0)),
            pl.BlockSpec((None, tn, d), lambda i, j: (0, j, 0)),
            pl.BlockSpec((None, LANES, d), lambda i, j: (layer, 0, 0)),
        ],
        out_specs=[
            pl.BlockSpec((tm, tn), lambda i, j: (i, j)),
            pl.BlockSpec((tm, LANES), lambda i, j: (i, 0)),
            pl.BlockSpec((tm, d), lambda i, j: (i, 0)),
        ],
        out_shape=[
            jax.ShapeDtypeStruct((n, MIX_COLS), BF16),
            jax.ShapeDtypeStruct((n, LANES), F32),
            jax.ShapeDtypeStruct((n, d), BF16),
        ],
        compiler_params=_params("arbitrary", "arbitrary"),
        name="inproj",
    )(x2, mod, gain, w_mix, w_forget)


def _attn_kernel(q_ref, k_ref, v_ref, zf_ref, bf_ref, qg_ref, kg_ref, *rest, n_cast, repack_first):
    n_repack = len(repack_first)
    n_in = n_cast + 2 * n_repack
    side_in, rest = rest[:n_in], rest[n_in:]
    o_ref, side_out = rest[0], rest[1:1 + n_cast + n_repack]
    cum_ref, qa_ref, ka_ref, va_ref, s_ref, p_ref = rest[1 + n_cast + n_repack:]
    for src, dst in zip(side_in[:n_cast], side_out[:n_cast]):
        dst[...] = src[...].astype(BF16)
    step = pl.program_id(0) * pl.num_programs(1) + pl.program_id(1)
    for r, first in enumerate(repack_first):
        main_ref, tail_ref = side_in[n_cast + 2 * r], side_in[n_cast + 2 * r + 1]
        side_out[n_cast + r][...] = _repacked_rows(main_ref[...], tail_ref[...],
                                                   (first + step) * main_ref.shape[0])

    seq = q_ref.shape[0]
    g = pl.program_id(1)

    @pl.when(g == 0)
    def _():
        logit = zf_ref[...] + bf_ref[...]
        log_f = jnp.minimum(logit, 0.0) - jnp.log1p(jnp.exp(-jnp.abs(logit)))
        acc = log_f.T[0:HEADS, :]
        lane = lax.broadcasted_iota(jnp.int32, acc.shape, 1)
        shift = 1
        while shift < seq:
            acc = acc + jnp.where(lane >= shift, pltpu.roll(acc, shift, axis=1), 0.0)
            shift *= 2
        pad = jnp.zeros((LANES - HEADS, seq), F32)
        rest = jnp.concatenate([acc * LOG2E, pad], axis=0).T
        for j in range(ATT_AUG):
            piece = rest.astype(BF16)
            cum_ref[:, j * LANES:(j + 1) * LANES] = piece
            rest = rest - piece.astype(F32)

    scale = 1.0 / math.sqrt(HEAD_DIM)
    lane = lax.broadcasted_iota(jnp.int32, (seq, LANES), 1)
    in_h0 = lane < HEAD_DIM

    def normalise(ref, gain_ref, mult):
        x = ref[...].astype(F32)
        sq = x * x
        ms0 = jnp.sum(jnp.where(in_h0, sq, 0.0), axis=1, keepdims=True)
        ms1 = jnp.sum(jnp.where(in_h0, 0.0, sq), axis=1, keepdims=True)
        inv = lax.rsqrt(jnp.where(in_h0, ms0, ms1) * (1.0 / HEAD_DIM) + RMS_EPS)
        return x * inv * (gain_ref[...] * mult)

    qn = normalise(q_ref, qg_ref, scale * LOG2E)
    kn = normalise(k_ref, kg_ref, 1.0)

    e_row = lax.broadcasted_iota(jnp.int32, (LANES, LANES), 0)
    e_col = lax.broadcasted_iota(jnp.int32, (LANES, LANES), 1)
    for hh in range(HEADS_PER_STEP):
        head = g * HEADS_PER_STEP + hh
        base = HEAD_DIM * (1 - hh)
        pick = e_row == head
        sel = []
        for j in range(ATT_AUG):
            plus = jnp.where(pick & (e_col == base + j), 1.0, 0.0)
            minus = jnp.where(pick & (e_col == base + ATT_AUG + j), 1.0, 0.0)
            sel.append(plus - minus)
        sel = jnp.concatenate(sel, axis=0).astype(BF16)
        aug = jnp.dot(cum_ref[...], sel, preferred_element_type=F32)
        in_head = in_h0 if hh == 0 else jnp.logical_not(in_h0)
        first = (lane >= base) & (lane < base + ATT_AUG)
        second = (lane >= base + ATT_AUG) & (lane < base + 2 * ATT_AUG)
        qa = jnp.where(in_head, qn, jnp.where(first, aug, jnp.where(second, 1.0, 0.0)))
        ka = jnp.where(in_head, kn, jnp.where(first, 1.0, jnp.where(second, aug, 0.0)))
        qa_ref[hh] = qa.astype(BF16)
        ka_ref[hh] = ka.astype(BF16)

    va_ref[:, 0:LANES] = v_ref[...]
    va_ref[:, LANES:] = jnp.ones((seq, LANES), BF16)

    bq = ATT_BQ
    row = lax.broadcasted_iota(jnp.int32, (bq, bq), 0)
    col = lax.broadcasted_iota(jnp.int32, (bq, bq), 1)
    causal = row >= col
    out_h0 = lax.broadcasted_iota(jnp.int32, (bq, LANES), 1) < HEAD_DIM
    nt = (((1,), (1,)), ((), ()))
    def scores(i, hh, slot):
        qa = qa_ref[hh, i * bq:(i + 1) * bq, :]
        m_part = None
        for c in range(i + 1):
            k0 = c * bq
            s = lax.dot_general(qa, ka_ref[hh, k0:k0 + bq, :], nt, preferred_element_type=F32)
            if c == i:
                s = jnp.where(causal, s, -jnp.inf)
            s_ref[slot, :, k0:k0 + bq] = s
            for h0 in range(0, bq, LANES):
                part = s[:, h0:h0 + LANES]
                m_part = part if m_part is None else jnp.maximum(m_part, part)
        return jnp.max(m_part, axis=1, keepdims=True)

    def weighted_values(i, slot, m):
        for c in range(i + 1):
            k0 = c * bq
            p = jnp.exp2(s_ref[slot, :, k0:k0 + bq] - m)
            p_ref[slot, :, k0:k0 + bq] = p.astype(BF16)
        acc = jnp.dot(p_ref[slot, :, 0:(i + 1) * bq], va_ref[0:(i + 1) * bq, :],
                      preferred_element_type=F32)
        return acc[:, 0:LANES] / acc[:, LANES:LANES + 1]

    units = [(i, hh) for i in range(seq // bq) for hh in range(HEADS_PER_STEP)]
    m_next = scores(*units[0], 0)
    outs = []
    for u, (i, hh) in enumerate(units):
        m_cur = m_next
        if u + 1 < len(units):
            m_next = scores(*units[u + 1], (u + 1) % 2)
        outs.append(weighted_values(i, u % 2, m_cur))
        if hh == HEADS_PER_STEP - 1:
            o_ref[i * bq:(i + 1) * bq, :] = jnp.where(out_h0, outs[0], outs[1]).astype(o_ref.dtype)
            outs = []


def _attention(z, zf, b_f_row, q_gain, k_gain, batch, seq, cast=(), cast_layer=0, repack=()):
    n = z.shape[0]
    groups = HEADS // HEADS_PER_STEP
    steps = batch * groups
    blk = lambda off: pl.BlockSpec((seq, LANES), lambda b, g, off=off: (b, off + g))
    slab = lambda w, layer: pl.BlockSpec((1, w.shape[1] // steps, w.shape[2]),
                                         lambda b, g: (layer, b * groups + g, 0))
    side_in, side_specs_in = list(cast), [slab(w, cast_layer) for w in cast]
    side_specs_out = [slab(w, 0) for w in cast]
    side_shapes = [jax.ShapeDtypeStruct((1,) + w.shape[1:], BF16) for w in cast]
    repack_first = []
    for w_t, layer, first_row, n_rows in repack:
        rows = n_rows // steps
        repack_first.append(first_row // rows)
        specs_in, spec_out = _repack_specs(layer, rows, first_row // rows,
                                           lambda b, g: b * groups + g)
        side_in += [w_t, w_t]
        side_specs_in += specs_in
        side_specs_out.append(spec_out)
        side_shapes.append(jax.ShapeDtypeStruct((1, n_rows, w_t.shape[2]), BF16))
    outs = pl.pallas_call(
        functools.partial(_attn_kernel, n_cast=len(cast), repack_first=tuple(repack_first)),
        grid=(batch, groups),
        in_specs=[
            blk(0), blk(groups), blk(2 * groups),
            pl.BlockSpec((seq, LANES), lambda b, g: (b, 0)),
            pl.BlockSpec((1, LANES), lambda b, g: (0, 0)),
            pl.BlockSpec((1, LANES), lambda b, g: (0, 0)),
            pl.BlockSpec((1, LANES), lambda b, g: (0, 0)),
        ] + side_specs_in,
        out_specs=[pl.BlockSpec((seq, LANES), lambda b, g: (b, g))] + side_specs_out,
        out_shape=[jax.ShapeDtypeStruct((n, WIDTH), BF16)] + side_shapes,
        scratch_shapes=[
            pltpu.VMEM((seq, ATT_AUG * LANES), BF16),
            pltpu.VMEM((HEADS_PER_STEP, seq, LANES), BF16),
            pltpu.VMEM((HEADS_PER_STEP, seq, LANES), BF16),
            pltpu.VMEM((seq, 2 * LANES), BF16),
            pltpu.VMEM((2, ATT_BQ, seq), F32),
            pltpu.VMEM((2, ATT_BQ, seq), BF16),
        ],
        compiler_params=_params("arbitrary", "arbitrary"),
        name="attn",
    )(z, z, z, zf, b_f_row, q_gain, k_gain, *side_in)
    return outs


def _conv_kernel(a_ref, g_ref, p_ref, x_ref, b_ref, c_ref,
                 ah_ref, gh_ref, ph_ref, xh_ref, ch_ref,
                 dw_ref, db_ref, lng_ref, lnb_ref, pw_ref, ps_ref, sw_ref,
                 o_ref, ubuf, pbuf, vbuf):
    tt = a_ref.shape[0]
    t = pl.program_id(1)
    has_past = t > 0
    H = CONV_HALO

    u_h = ah_ref[...].astype(F32) * _sigmoid(gh_ref[...].astype(F32))
    ubuf[0:H, :] = jnp.where(has_past, u_h, 0.0)
    ubuf[H:, :] = a_ref[...].astype(F32) * _sigmoid(g_ref[...].astype(F32))
    pbuf[0:H, :] = jnp.where(has_past, ph_ref[...].astype(F32), 0.0)
    pbuf[H:, :] = p_ref[...].astype(F32)
    v_h = ch_ref[...].astype(F32) * xh_ref[...].astype(F32)
    vbuf[0:H, :] = jnp.where(has_past, v_h, 0.0)
    vbuf[H:, :] = c_ref[...].astype(F32) * x_ref[...].astype(F32)

    R = CONV_ROWS

    def chunk(ci, carry):
        r0 = pl.multiple_of(ci * R, R)

        acc = jnp.broadcast_to(db_ref[...], (R, WIDTH))
        win = ubuf[pl.ds(r0, R + H), :]
        for r in range(SUBLANES):
            rolled = win if r == 0 else pltpu.roll(win, r, axis=0)
            for a in range((CONF_K - 1 - r) // SUBLANES + 1):
                k = CONF_K - 1 - (SUBLANES * a + r)
                lo = H - SUBLANES * a
                acc = acc + dw_ref[k:k + 1, :] * rolled[lo:lo + R, :]
        mu = jnp.mean(acc, axis=-1, keepdims=True)
        xc = acc - mu
        y = xc * lax.rsqrt(jnp.mean(xc * xc, axis=-1, keepdims=True) + LN_EPS)
        y = y * lng_ref[...] + lnb_ref[...]
        o_ref[pl.ds(r0, R), 0:WIDTH] = (y * _sigmoid(y)).astype(o_ref.dtype)

        pos = t * tt + r0 + lax.broadcasted_iota(jnp.int32, (R, 1), 0)
        pooled = []
        for gi, w in enumerate(POOL_WINDOWS):
            lo = gi * POOL_GROUP
            tot = pbuf[pl.ds(r0 + (H - POOL_HIST), R + POOL_HIST), lo:lo + POOL_GROUP]
            tok = tot[POOL_HIST:, :]
            span = 1
            while span < w:
                tot = tot + pltpu.roll(tot, span, axis=0)
                span *= 2
            cnt = jnp.minimum(pos + 1, w).astype(F32)
            dlt = (tot[POOL_HIST:, :] / cnt - tok).astype(BF16)
            pooled.append(jnp.dot(dlt, pw_ref[gi], preferred_element_type=F32))
        yp = jnp.concatenate(pooled, axis=1) * ps_ref[...]
        o_ref[pl.ds(r0, R), WIDTH:2 * WIDTH] = yp.astype(o_ref.dtype)

        vwin = vbuf[pl.ds(r0 + (H - SUBLANES), R + SUBLANES), :]
        sc = sw_ref[SHORT_K - 1:SHORT_K, :] * vwin[SUBLANES:, :]
        for dly in range(1, SHORT_K):
            k = SHORT_K - 1 - dly
            sc = sc + sw_ref[k:k + 1, :] * pltpu.roll(vwin, dly, axis=0)[SUBLANES:, :]
        ys = b_ref[pl.ds(r0, R), :].astype(F32) * sc
        o_ref[pl.ds(r0, R), 2 * WIDTH:3 * WIDTH] = ys.astype(o_ref.dtype)
        return carry

    lax.fori_loop(0, tt // R, chunk, 0, unroll=2)


def _conv_mixers(z, conf_dw, conf_db, ln_g, ln_b, pool_w, layer, pool_scale, sconv_w, batch, seq):
    n = z.shape[0]
    tt, H = CONV_TT, CONV_HALO
    tiles = seq // tt
    main = lambda cb: pl.BlockSpec((tt, WIDTH), lambda b, t, cb=cb: (b * tiles + t, cb))
    halo = lambda cb: pl.BlockSpec(
        (H, WIDTH),
        lambda b, t, cb=cb: (jnp.maximum((b * tiles + t) * (tt // H) - 1, 0), cb))
    full = lambda shape: pl.BlockSpec(shape, lambda b, t: (0,) * len(shape))
    return pl.pallas_call(
        _conv_kernel,
        grid=(batch, tiles),
        in_specs=[
            main(3), main(4), main(5), main(6), main(7), main(8),
            halo(3), halo(4), halo(5), halo(6), halo(8),
            full((CONF_K, WIDTH)), full((1, WIDTH)), full((1, WIDTH)), full((1, WIDTH)),
            pl.BlockSpec((None, len(POOL_WINDOWS), POOL_GROUP, POOL_GROUP),
                         lambda b, t: (layer, 0, 0, 0)),
            full((1, WIDTH)),
            full((SHORT_K, WIDTH)),
        ],
        out_specs=pl.BlockSpec((tt, 3 * WIDTH), lambda b, t: (b * tiles + t, 0)),
        out_shape=jax.ShapeDtypeStruct((n, 3 * WIDTH), BF16),
        scratch_shapes=[pltpu.VMEM((H + tt, WIDTH), F32)] * 3,
        compiler_params=_params("arbitrary", "arbitrary"),
        name="conv",
    )(z, z, z, z, z, z, z, z, z, z, z,
      conf_dw, conf_db, ln_g, ln_b, pool_w, pool_scale, sconv_w)


def _gate_kernel(h_ref, ya_ref, yr_ref,
                 wg0_ref, wg1_ref, wg2_ref, wg3_ref, bg0_ref, bg1_ref, bg2_ref, bg3_ref,
                 wbr_ref, o_ref):
    h = h_ref[...]
    wgs = (wg0_ref, wg1_ref, wg2_ref, wg3_ref)
    bgs = (bg0_ref, bg1_ref, bg2_ref, bg3_ref)
    merged = None
    for b in range(4):
        y = ya_ref[...] if b == 0 else yr_ref[:, (b - 1) * WIDTH:b * WIDTH]
        logit = lax.dot_general(h, wgs[b][...], _NT, preferred_element_type=F32)
        gate = _sigmoid(logit + bgs[b][...])
        term = gate * jnp.dot(y, wbr_ref[b], preferred_element_type=F32)
        merged = term if merged is None else merged + term
    o_ref[...] = merged.astype(BF16)


def _gated_merge(h, y_att, y_rest, w_gate, b_gate, w_branch, layer):
    n, d = h.shape
    tm, tn = GATE_TM, GATE_TN
    na = d // tn
    wg = lambda b: pl.BlockSpec((None, tn, d), lambda i, j, b=b: (0, b * na + j, 0))
    bg = lambda b: pl.BlockSpec((None, 1, tn), lambda i, j, b=b: (layer, 0, b * na + j))
    return pl.pallas_call(
        _gate_kernel,
        grid=(n // tm, na),
        in_specs=[
            pl.BlockSpec((tm, d), lambda i, j: (i, 0)),
            pl.BlockSpec((tm, WIDTH), lambda i, j: (i, 0)),
            pl.BlockSpec((tm, 3 * WIDTH), lambda i, j: (i, 0)),
            wg(0), wg(1), wg(2), wg(3), bg(0), bg(1), bg(2), bg(3),
            pl.BlockSpec((None, 4, WIDTH, tn), lambda i, j: (0, 0, 0, j)),
        ],
        out_specs=pl.BlockSpec((tm, tn), lambda i, j: (i, j)),
        out_shape=jax.ShapeDtypeStruct((n, d), BF16),
        compiler_params=_params("arbitrary", "arbitrary"),
        name="gate",
    )(h, y_att, y_rest, w_gate, w_gate, w_gate, w_gate,
      b_gate, b_gate, b_gate, b_gate, w_branch)


def _outproj_kernel(m_ref, x_ref, mod_ref, wout_ref, o_ref):
    proj = jnp.dot(m_ref[...], wout_ref[...], preferred_element_type=F32)
    o_ref[...] = x_ref[...] + mod_ref[2:3, :] * proj


def _outproj(merged, x2, mod, w_out, seq):
    n, d = x2.shape
    tm = OUTPROJ_TM
    per_batch = seq // tm
    return pl.pallas_call(
        _outproj_kernel,
        grid=(n // tm,),
        in_specs=[
            pl.BlockSpec((tm, d), lambda i: (i, 0)),
            pl.BlockSpec((tm, d), lambda i: (i, 0)),
            pl.BlockSpec((None, 6, d), lambda i: (i // per_batch, 0, 0)),
            pl.BlockSpec((None, d, d), lambda i: (0, 0, 0), pipeline_mode=pl.Buffered(1)),
        ],
        out_specs=pl.BlockSpec((tm, d), lambda i: (i, 0)),
        out_shape=jax.ShapeDtypeStruct((n, d), F32),
        compiler_params=_params("arbitrary"),
        name="outproj",
    )(merged, x2, mod, w_out)


def _mlp_kernel(x_ref, mod_ref, gain_ref, w1_ref, w2_ref, o_ref, h_ref):
    j = pl.program_id(1)

    @pl.when(j == 0)
    def _():
        x = x_ref[...]
        h = _modulated_rmsnorm(x, gain_ref[...], mod_ref[3:4, :], mod_ref[4:5, :])
        h_ref[...] = h.astype(BF16)
        o_ref[...] = x

    hid = jnp.maximum(jnp.dot(h_ref[...], w1_ref[...], preferred_element_type=F32), 0.0)
    part = jnp.dot((hid * hid).astype(BF16), w2_ref[...], preferred_element_type=F32)
    o_ref[...] += mod_ref[5:6, :] * part


def _mlp(x2, mod, gain, w1, w2, seq):
    n, d = x2.shape
    tm, tf = MLP_TM, MLP_TF
    per_batch = seq // tm
    return pl.pallas_call(
        _mlp_kernel,
        grid=(n // tm, D_FF // tf),
        in_specs=[
            pl.BlockSpec((tm, d), lambda i, j: (i, 0)),
            pl.BlockSpec((None, 6, d), lambda i, j: (i // per_batch, 0, 0)),
            pl.BlockSpec((1, d), lambda i, j: (0, 0)),
            pl.BlockSpec((None, d, tf), lambda i, j: (0, 0, j)),
            pl.BlockSpec((None, tf, d), lambda i, j: (0, j, 0)),
        ],
        out_specs=pl.BlockSpec((tm, d), lambda i, j: (i, 0)),
        out_shape=jax.ShapeDtypeStruct((n, d), F32),
        scratch_shapes=[pltpu.VMEM((tm, d), BF16)],
        compiler_params=_params("arbitrary", "arbitrary"),
        name="mlp",
    )(x2, mod, gain, w1, w2)


def kernel(x, c, w_ada, b_ada, norm_gain, w_in, b_f, b_gate, q_gain, k_gain, conf_dw, conf_db,
           conf_ln_g, conf_ln_b, pool_w, pool_scale, sconv_w, w_branch, w_out, w_mlp1, w_mlp2):
    batch, seq, d = x.shape
    depth = w_ada.shape[0]
    W = WIDTH
    att_cols = 3 * W + HEADS

    w_t = jnp.swapaxes(w_in, 1, 2)
    w_mix = _repack_mixer_rows(w_t, 0)
    w_forget = jnp.pad(w_t[:, 3 * W:att_cols], ((0, 0), (0, LANES - HEADS), (0, 0))).astype(BF16)
    pool_w_b = pool_w.astype(BF16)
    b_gate3 = b_gate.reshape(depth, 1, 4 * d)

    mod_all = _ada(c, w_ada, b_ada).reshape(depth, batch, 6, d)
    x2 = x.reshape(batch * seq, d)
    for l in range(depth):
        mod = mod_all[l]
        b_f_row = jnp.pad(b_f[l], (0, LANES - HEADS)).reshape(1, LANES)

        z, zf, h = _inproj(x2, mod, norm_gain[l, 0:1], w_mix, w_forget, l, seq)
        y_rest = _conv_mixers(z, conf_dw[l], conf_db[l].reshape(1, W), conf_ln_g[l].reshape(1, W),
                              conf_ln_b[l].reshape(1, W), pool_w_b, l,
                              pool_scale[l].reshape(1, W), sconv_w[l], batch, seq)
        repack = [(w_t, l, MIX_COLS, GATE_COLS)]
        if l + 1 < depth:
            repack.append((w_t, l + 1, 0, MIX_COLS))
        y_att, *side = _attention(z, zf, b_f_row,
                                  jnp.tile(q_gain[l], HEADS_PER_STEP).reshape(1, LANES),
                                  jnp.tile(k_gain[l], HEADS_PER_STEP).reshape(1, LANES), batch, seq,
                                  cast=(w_mlp1, w_mlp2, w_out, w_branch.reshape(depth, 4 * W, d)),
                                  cast_layer=l, repack=repack)
        w_mlp1_b, w_mlp2_b, w_out_b, w_branch_b, w_gate = side[:5]
        w_mix = side[5] if l + 1 < depth else None
        w_branch_b = w_branch_b.reshape(1, 4, W, d)
        merged = _gated_merge(h, y_att, y_rest, w_gate, b_gate3, w_branch_b, l)
        x2 = _outproj(merged, x2, mod, w_out_b, seq)
        x2 = _mlp(x2, mod, norm_gain[l, 1:2], w_mlp1_b, w_mlp2_b, seq)
    return x2.reshape(batch, seq, d)
```

```python
import functools
import math

import jax
import jax.numpy as jnp
from jax import lax
from jax.experimental import pallas as pl
from jax.experimental.pallas import tpu as pltpu

F32 = jnp.float32
BF16 = jnp.bfloat16

D_MODEL = 2048
WIDTH = D_MODEL // 4
HEADS = 8
HEAD_DIM = WIDTH // HEADS
CONF_K = 31
POOL_WINDOWS = (2, 4, 8, 16)
POOL_GROUP = WIDTH // len(POOL_WINDOWS)
SHORT_K = 3
D_FF = 4 * D_MODEL
RMS_EPS = 1e-6
LN_EPS = 1e-5

LANES = 128
SUBLANES = 8
VMEM_LIMIT = 60 * 1024 * 1024

ADA_TK = 256
ADA_SPLIT = 4
INPROJ_TM = 1024
INPROJ_TN = 2304
ATT_BQ = 256
HEADS_PER_STEP = LANES // HEAD_DIM
ATT_AUG = 3
LOG2E = math.log2(math.e)
CONV_TT = 1024
CONV_HALO = 32
CONV_ROWS = 32
POOL_HIST = 16
GATE_TM = 2048
GATE_TN = 256
OUTPROJ_TM = 1024
MLP_TM = 1024
MLP_TF = 512

MIX_COLS = 9 * WIDTH
GATE_COLS = 4 * D_MODEL
ATT_COLS = 3 * WIDTH
REPACK_ROWS = 512


def _params(*sem):
    return pltpu.CompilerParams(dimension_semantics=sem, vmem_limit_bytes=VMEM_LIMIT)


def _sigmoid(x):
    return 1.0 / (1.0 + jnp.exp(-x))


def _modulated_rmsnorm(x, gain, shift, scale):
    inv = lax.rsqrt(jnp.mean(x * x, axis=-1, keepdims=True) + RMS_EPS)
    return x * inv * (gain * (1.0 + scale)) + shift


_NT = (((1,), (1,)), ((), ()))


def _repacked_rows(main, tail, first_row):
    shifted = jnp.concatenate([main[HEADS:, :], tail], axis=0)
    row = first_row + lax.broadcasted_iota(jnp.int32, (main.shape[0], 1), 0)
    return jnp.where(row >= ATT_COLS, shifted, main).astype(BF16)


def _repack_specs(layer, rows, first, index):
    main = pl.BlockSpec((None, rows, D_MODEL), lambda *g: (layer, first + index(*g), 0))
    tail = pl.BlockSpec((None, HEADS, D_MODEL),
                        lambda *g: (layer, (first + index(*g) + 1) * (rows // HEADS), 0))
    out = pl.BlockSpec((None, rows, D_MODEL), lambda *g: (0, index(*g), 0))
    return [main, tail], out


def _repack_kernel(w_ref, tail_ref, o_ref):
    o_ref[...] = _repacked_rows(w_ref[...], tail_ref[...], pl.program_id(0) * w_ref.shape[0])


def _repack_mixer_rows(w_t, layer):
    in_specs, out_spec = _repack_specs(layer, REPACK_ROWS, 0, lambda r: r)
    return pl.pallas_call(
        _repack_kernel,
        grid=(MIX_COLS // REPACK_ROWS,),
        in_specs=in_specs,
        out_specs=out_spec,
        out_shape=jax.ShapeDtypeStruct((1, MIX_COLS, w_t.shape[2]), BF16),
        compiler_params=_params("arbitrary"),
        name="repack",
    )(w_t, w_t)


def _ada_kernel(c_ref, *refs):
    w_refs, b_ref, o_ref = refs[:ADA_SPLIT], refs[ADA_SPLIT], refs[ADA_SPLIT + 1]

    @pl.when(pl.program_id(1) == 0)
    def _():
        o_ref[...] = jnp.broadcast_to(b_ref[...], o_ref.shape)

    cb = c_ref[...].astype(BF16)
    part = o_ref.shape[1] // ADA_SPLIT
    for s, w_ref in enumerate(w_refs):
        o_ref[:, s * part:(s + 1) * part] += jnp.dot(cb, w_ref[...].astype(BF16),
                                                     preferred_element_type=F32)


def _ada(c, w_ada, b_ada):
    depth, d, n = w_ada.shape
    b = c.shape[0]
    part = n // ADA_SPLIT
    w_spec = lambda s: pl.BlockSpec((None, ADA_TK, part), lambda l, k, s=s: (l, k, s))
    return pl.pallas_call(
        _ada_kernel,
        grid=(depth, d // ADA_TK),
        in_specs=[pl.BlockSpec((b, ADA_TK), lambda l, k: (0, k))]
        + [w_spec(s) for s in range(ADA_SPLIT)]
        + [pl.BlockSpec((None, 1, n), lambda l, k: (l, 0, 0))],
        out_specs=pl.BlockSpec((None, b, n), lambda l, k: (l, 0, 0)),
        out_shape=jax.ShapeDtypeStruct((depth, b, n), F32),
        compiler_params=_params("arbitrary", "arbitrary"),
        name="ada",
    )(c, *([w_ada] * ADA_SPLIT), b_ada.reshape(depth, 1, n))


def _inproj_kernel(x_ref, mod_ref, gain_ref, w_ref, wf_ref, z_ref, zf_ref, h_ref):
    @pl.when(pl.program_id(1) == 0)
    def _():
        h = _modulated_rmsnorm(x_ref[...], gain_ref[...], mod_ref[0:1, :], mod_ref[1:2, :])
        hb = h.astype(BF16)
        h_ref[...] = hb
        zf_ref[...] = lax.dot_general(hb, wf_ref[...], _NT, preferred_element_type=F32)

    z_ref[...] = lax.dot_general(h_ref[...], w_ref[...], _NT,
                                 preferred_element_type=F32).astype(z_ref.dtype)


def _inproj(x2, mod, gain, w_mix, w_forget, layer, seq):
    n, d = x2.shape
    tm, tn = INPROJ_TM, INPROJ_TN
    per_batch = seq // tm
    return pl.pallas_call(
        _inproj_kernel,
        grid=(n // tm, MIX_COLS // tn),
        in_specs=[
            pl.BlockSpec((tm, d), lambda i, j: (i, 0)),
            pl.BlockSpec((None, 6, d), lambda i, j: (i // per_batch, 0, 0)),
            pl.BlockSpec((1, d), lambda i, j: (0, 0)),
            pl.BlockSpec((None, tn, d), lambda i, j: (0, j, 0)),
            pl.BlockSpec((None, LANES, d), lambda i, j: (layer, 0, 0)),
        ],
        out_specs=[
            pl.BlockSpec((tm, tn), lambda i, j: (i, j)),
            pl.BlockSpec((tm, LANES), lambda i, j: (i, 0)),
            pl.BlockSpec((tm, d), lambda i, j: (i, 0)),
        ],
        out_shape=[
            jax.ShapeDtypeStruct((n, MIX_COLS), BF16),
            jax.ShapeDtypeStruct((n, LANES), F32),
            jax.ShapeDtypeStruct((n, d), BF16),
        ],
        compiler_params=_params("arbitrary", "arbitrary"),
        name="inproj",
    )(x2, mod, gain, w_mix, w_forget)


def _attn_kernel(q_ref, k_ref, v_ref, zf_ref, bf_ref, qg_ref, kg_ref, *rest, n_cast, repack_first):
    n_repack = len(repack_first)
    n_in = n_cast + 2 * n_repack
    side_in, rest = rest[:n_in], rest[n_in:]
    o_ref, side_out = rest[0], rest[1:1 + n_cast + n_repack]
    cum_ref, qa_ref, ka_ref, s_ref, p_ref = rest[1 + n_cast + n_repack:]
    for src, dst in zip(side_in[:n_cast], side_out[:n_cast]):
        dst[...] = src[...].astype(BF16)
    step = pl.program_id(0) * pl.num_programs(1) + pl.program_id(1)
    for r, first in enumerate(repack_first):
        main_ref, tail_ref = side_in[n_cast + 2 * r], side_in[n_cast + 2 * r + 1]
        side_out[n_cast + r][...] = _repacked_rows(main_ref[...], tail_ref[...],
                                                   (first + step) * main_ref.shape[0])

    seq = q_ref.shape[0]
    g = pl.program_id(1)

    @pl.when(g == 0)
    def _():
        logit = zf_ref[...] + bf_ref[...]
        log_f = jnp.minimum(logit, 0.0) - jnp.log1p(jnp.exp(-jnp.abs(logit)))
        acc = log_f.T[0:HEADS, :]
        lane = lax.broadcasted_iota(jnp.int32, acc.shape, 1)
        shift = 1
        while shift < seq:
            acc = acc + jnp.where(lane >= shift, pltpu.roll(acc, shift, axis=1), 0.0)
            shift *= 2
        pad = jnp.zeros((LANES - HEADS, seq), F32)
        rest = jnp.concatenate([acc * LOG2E, pad], axis=0).T
        for j in range(ATT_AUG):
            piece = rest.astype(BF16)
            cum_ref[:, j * LANES:(j + 1) * LANES] = piece
            rest = rest - piece.astype(F32)

    scale = 1.0 / math.sqrt(HEAD_DIM)
    lane = lax.broadcasted_iota(jnp.int32, (seq, LANES), 1)
    in_h0 = lane < HEAD_DIM

    def normalise(ref, gain_ref, mult):
        x = ref[...].astype(F32)
        sq = x * x
        ms0 = jnp.sum(jnp.where(in_h0, sq, 0.0), axis=1, keepdims=True)
        ms1 = jnp.sum(jnp.where(in_h0, 0.0, sq), axis=1, keepdims=True)
        inv = lax.rsqrt(jnp.where(in_h0, ms0, ms1) * (1.0 / HEAD_DIM) + RMS_EPS)
        return x * inv * (gain_ref[...] * mult)

    qn = normalise(q_ref, qg_ref, scale * LOG2E)
    kn = normalise(k_ref, kg_ref, 1.0)

    e_row = lax.broadcasted_iota(jnp.int32, (LANES, LANES), 0)
    e_col = lax.broadcasted_iota(jnp.int32, (LANES, LANES), 1)
    for hh in range(HEADS_PER_STEP):
        head = g * HEADS_PER_STEP + hh
        base = HEAD_DIM * (1 - hh)
        pick = e_row == head
        sel = []
        for j in range(ATT_AUG):
            plus = jnp.where(pick & (e_col == base + j), 1.0, 0.0)
            minus = jnp.where(pick & (e_col == base + ATT_AUG + j), 1.0, 0.0)
            sel.append(plus - minus)
        sel = jnp.concatenate(sel, axis=0).astype(BF16)
        aug = jnp.dot(cum_ref[...], sel, preferred_element_type=F32)
        in_head = in_h0 if hh == 0 else jnp.logical_not(in_h0)
        first = (lane >= base) & (lane < base + ATT_AUG)
        second = (lane >= base + ATT_AUG) & (lane < base + 2 * ATT_AUG)
        qa = jnp.where(in_head, qn, jnp.where(first, aug, jnp.where(second, 1.0, 0.0)))
        ka = jnp.where(in_head, kn, jnp.where(first, 1.0, jnp.where(second, aug, 0.0)))
        qa_ref[hh] = qa.astype(BF16)
        ka_ref[hh] = ka.astype(BF16)

    bq = ATT_BQ
    row = lax.broadcasted_iota(jnp.int32, (bq, bq), 0)
    col = lax.broadcasted_iota(jnp.int32, (bq, bq), 1)
    causal = row >= col
    out_h0 = lax.broadcasted_iota(jnp.int32, (bq, LANES), 1) < HEAD_DIM
    nt = (((1,), (1,)), ((), ()))
    def scores(i, hh, slot):
        qa = qa_ref[hh, i * bq:(i + 1) * bq, :]
        m_part = None
        for c in range(i + 1):
            k0 = c * bq
            s = lax.dot_general(qa, ka_ref[hh, k0:k0 + bq, :], nt, preferred_element_type=F32)
            if c == i:
                s = jnp.where(causal, s, -jnp.inf)
            s_ref[slot, :, k0:k0 + bq] = s
            for h0 in range(0, bq, LANES):
                part = s[:, h0:h0 + LANES]
                m_part = part if m_part is None else jnp.maximum(m_part, part)
        return jnp.max(m_part, axis=1, keepdims=True)

    def weighted_values(i, slot, m):
        l_part = None
        for c in range(i + 1):
            k0 = c * bq
            p = jnp.exp2(s_ref[slot, :, k0:k0 + bq] - m)
            for h0 in range(0, bq, LANES):
                part = p[:, h0:h0 + LANES]
                l_part = part if l_part is None else l_part + part
            p_ref[slot, :, k0:k0 + bq] = p.astype(BF16)
        acc = jnp.dot(p_ref[slot, :, 0:(i + 1) * bq], v_ref[0:(i + 1) * bq, :],
                      preferred_element_type=F32)
        return acc / jnp.sum(l_part, axis=1, keepdims=True)

    units = [(i, hh) for i in range(seq // bq) for hh in range(HEADS_PER_STEP)]
    m_next = scores(*units[0], 0)
    outs = []
    for u, (i, hh) in enumerate(units):
        m_cur = m_next
        if u + 1 < len(units):
            m_next = scores(*units[u + 1], (u + 1) % 2)
        outs.append(weighted_values(i, u % 2, m_cur))
        if hh == HEADS_PER_STEP - 1:
            o_ref[i * bq:(i + 1) * bq, :] = jnp.where(out_h0, outs[0], outs[1]).astype(o_ref.dtype)
            outs = []


def _attention(z, zf, b_f_row, q_gain, k_gain, batch, seq, cast=(), cast_layer=0, repack=()):
    n = z.shape[0]
    groups = HEADS // HEADS_PER_STEP
    steps = batch * groups
    blk = lambda off: pl.BlockSpec((seq, LANES), lambda b, g, off=off: (b, off + g))
    slab = lambda w, layer: pl.BlockSpec((1, w.shape[1] // steps, w.shape[2]),
                                         lambda b, g: (layer, b * groups + g, 0))
    side_in, side_specs_in = list(cast), [slab(w, cast_layer) for w in cast]
    side_specs_out = [slab(w, 0) for w in cast]
    side_shapes = [jax.ShapeDtypeStruct((1,) + w.shape[1:], BF16) for w in cast]
    repack_first = []
    for w_t, layer, first_row, n_rows in repack:
        rows = n_rows // steps
        repack_first.append(first_row // rows)
        specs_in, spec_out = _repack_specs(layer, rows, first_row // rows,
                                           lambda b, g: b * groups + g)
        side_in += [w_t, w_t]
        side_specs_in += specs_in
        side_specs_out.append(spec_out)
        side_shapes.append(jax.ShapeDtypeStruct((1, n_rows, w_t.shape[2]), BF16))
    outs = pl.pallas_call(
        functools.partial(_attn_kernel, n_cast=len(cast), repack_first=tuple(repack_first)),
        grid=(batch, groups),
        in_specs=[
            blk(0), blk(groups), blk(2 * groups),
            pl.BlockSpec((seq, LANES), lambda b, g: (b, 0)),
            pl.BlockSpec((1, LANES), lambda b, g: (0, 0)),
            pl.BlockSpec((1, LANES), lambda b, g: (0, 0)),
            pl.BlockSpec((1, LANES), lambda b, g: (0, 0)),
        ] + side_specs_in,
        out_specs=[pl.BlockSpec((seq, LANES), lambda b, g: (b, g))] + side_specs_out,
        out_shape=[jax.ShapeDtypeStruct((n, WIDTH), BF16)] + side_shapes,
        scratch_shapes=[
            pltpu.VMEM((seq, ATT_AUG * LANES), BF16),
            pltpu.VMEM((HEADS_PER_STEP, seq, LANES), BF16),
            pltpu.VMEM((HEADS_PER_STEP, seq, LANES), BF16),
            pltpu.VMEM((2, ATT_BQ, seq), F32),
            pltpu.VMEM((2, ATT_BQ, seq), BF16),
        ],
        compiler_params=_params("arbitrary", "arbitrary"),
        name="attn",
    )(z, z, z, zf, b_f_row, q_gain, k_gain, *side_in)
    return outs


def _conv_kernel(a_ref, g_ref, p_ref, x_ref, b_ref, c_ref,
                 ah_ref, gh_ref, ph_ref, xh_ref, ch_ref,
                 dw_ref, db_ref, lng_ref, lnb_ref, pw_ref, ps_ref, sw_ref,
                 o_ref, ubuf, pbuf, vbuf):
    tt = a_ref.shape[0]
    t = pl.program_id(1)
    has_past = t > 0
    H = CONV_HALO

    u_h = ah_ref[...].astype(F32) * _sigmoid(gh_ref[...].astype(F32))
    ubuf[0:H, :] = jnp.where(has_past, u_h, 0.0)
    ubuf[H:, :] = a_ref[...].astype(F32) * _sigmoid(g_ref[...].astype(F32))
    pbuf[0:H, :] = jnp.where(has_past, ph_ref[...].astype(F32), 0.0)
    pbuf[H:, :] = p_ref[...].astype(F32)
    v_h = ch_ref[...].astype(F32) * xh_ref[...].astype(F32)
    vbuf[0:H, :] = jnp.where(has_past, v_h, 0.0)
    vbuf[H:, :] = c_ref[...].astype(F32) * x_ref[...].astype(F32)

    R = CONV_ROWS

    def chunk(ci, carry):
        r0 = pl.multiple_of(ci * R, R)

        acc = jnp.broadcast_to(db_ref[...], (R, WIDTH))
        win = ubuf[pl.ds(r0, R + H), :]
        for r in range(SUBLANES):
            rolled = win if r == 0 else pltpu.roll(win, r, axis=0)
            for a in range((CONF_K - 1 - r) // SUBLANES + 1):
                k = CONF_K - 1 - (SUBLANES * a + r)
                lo = H - SUBLANES * a
                acc = acc + dw_ref[k:k + 1, :] * rolled[lo:lo + R, :]
        mu = jnp.mean(acc, axis=-1, keepdims=True)
        xc = acc - mu
        y = xc * lax.rsqrt(jnp.mean(xc * xc, axis=-1, keepdims=True) + LN_EPS)
        y = y * lng_ref[...] + lnb_ref[...]
        o_ref[pl.ds(r0, R), 0:WIDTH] = (y * _sigmoid(y)).astype(o_ref.dtype)

        pos = t * tt + r0 + lax.broadcasted_iota(jnp.int32, (R, 1), 0)
        pooled = []
        for gi, w in enumerate(POOL_WINDOWS):
            lo = gi * POOL_GROUP
            tot = pbuf[pl.ds(r0 + (H - POOL_HIST), R + POOL_HIST), lo:lo + POOL_GROUP]
            tok = tot[POOL_HIST:, :]
            span = 1
            while span < w:
                tot = tot + pltpu.roll(tot, span, axis=0)
                span *= 2
            cnt = jnp.minimum(pos + 1, w).astype(F32)
            dlt = (tot[POOL_HIST:, :] / cnt - tok).astype(BF16)
            pooled.append(jnp.dot(dlt, pw_ref[gi], preferred_element_type=F32))
        yp = jnp.concatenate(pooled, axis=1) * ps_ref[...]
        o_ref[pl.ds(r0, R), WIDTH:2 * WIDTH] = yp.astype(o_ref.dtype)

        vwin = vbuf[pl.ds(r0 + (H - SUBLANES), R + SUBLANES), :]
        sc = sw_ref[SHORT_K - 1:SHORT_K, :] * vwin[SUBLANES:, :]
        for dly in range(1, SHORT_K):
            k = SHORT_K - 1 - dly
            sc = sc + sw_ref[k:k + 1, :] * pltpu.roll(vwin, dly, axis=0)[SUBLANES:, :]
        ys = b_ref[pl.ds(r0, R), :].astype(F32) * sc
        o_ref[pl.ds(r0, R), 2 * WIDTH:3 * WIDTH] = ys.astype(o_ref.dtype)
        return carry

    lax.fori_loop(0, tt // R, chunk, 0, unroll=2)


def _conv_mixers(z, conf_dw, conf_db, ln_g, ln_b, pool_w, layer, pool_scale, sconv_w, batch, seq):
    n = z.shape[0]
    tt, H = CONV_TT, CONV_HALO
    tiles = seq // tt
    main = lambda cb: pl.BlockSpec((tt, WIDTH), lambda b, t, cb=cb: (b * tiles + t, cb))
    halo = lambda cb: pl.BlockSpec(
        (H, WIDTH),
        lambda b, t, cb=cb: (jnp.maximum((b * tiles + t) * (tt // H) - 1, 0), cb))
    full = lambda shape: pl.BlockSpec(shape, lambda b, t: (0,) * len(shape))
    return pl.pallas_call(
        _conv_kernel,
        grid=(batch, tiles),
        in_specs=[
            main(3), main(4), main(5), main(6), main(7), main(8),
            halo(3), halo(4), halo(5), halo(6), halo(8),
            full((CONF_K, WIDTH)), full((1, WIDTH)), full((1, WIDTH)), full((1, WIDTH)),
            pl.BlockSpec((None, len(POOL_WINDOWS), POOL_GROUP, POOL_GROUP),
                         lambda b, t: (layer, 0, 0, 0)),
            full((1, WIDTH)),
            full((SHORT_K, WIDTH)),
        ],
        out_specs=pl.BlockSpec((tt, 3 * WIDTH), lambda b, t: (b * tiles + t, 0)),
        out_shape=jax.ShapeDtypeStruct((n, 3 * WIDTH), BF16),
        scratch_shapes=[pltpu.VMEM((H + tt, WIDTH), F32)] * 3,
        compiler_params=_params("arbitrary", "arbitrary"),
        name="conv",
    )(z, z, z, z, z, z, z, z, z, z, z,
      conf_dw, conf_db, ln_g, ln_b, pool_w, pool_scale, sconv_w)


def _gate_kernel(h_ref, ya_ref, yr_ref,
                 wg0_ref, wg1_ref, wg2_ref, wg3_ref, bg0_ref, bg1_ref, bg2_ref, bg3_ref,
                 wbr_ref, o_ref):
    h = h_ref[...]
    wgs = (wg0_ref, wg1_ref, wg2_ref, wg3_ref)
    bgs = (bg0_ref, bg1_ref, bg2_ref, bg3_ref)
    merged = None
    for b in range(4):
        y = ya_ref[...] if b == 0 else yr_ref[:, (b - 1) * WIDTH:b * WIDTH]
        logit = lax.dot_general(h, wgs[b][...], _NT, preferred_element_type=F32)
        gate = _sigmoid(logit + bgs[b][...])
        term = gate * jnp.dot(y, wbr_ref[b], preferred_element_type=F32)
        merged = term if merged is None else merged + term
    o_ref[...] = merged.astype(BF16)


def _gated_merge(h, y_att, y_rest, w_gate, b_gate, w_branch, layer):
    n, d = h.shape
    tm, tn = GATE_TM, GATE_TN
    na = d // tn
    wg = lambda b: pl.BlockSpec((None, tn, d), lambda i, j, b=b: (0, b * na + j, 0))
    bg = lambda b: pl.BlockSpec((None, 1, tn), lambda i, j, b=b: (layer, 0, b * na + j))
    return pl.pallas_call(
        _gate_kernel,
        grid=(n // tm, na),
        in_specs=[
            pl.BlockSpec((tm, d), lambda i, j: (i, 0)),
            pl.BlockSpec((tm, WIDTH), lambda i, j: (i, 0)),
            pl.BlockSpec((tm, 3 * WIDTH), lambda i, j: (i, 0)),
            wg(0), wg(1), wg(2), wg(3), bg(0), bg(1), bg(2), bg(3),
            pl.BlockSpec((None, 4, WIDTH, tn), lambda i, j: (0, 0, 0, j)),
        ],
        out_specs=pl.BlockSpec((tm, tn), lambda i, j: (i, j)),
        out_shape=jax.ShapeDtypeStruct((n, d), BF16),
        compiler_params=_params("arbitrary", "arbitrary"),
        name="gate",
    )(h, y_att, y_rest, w_gate, w_gate, w_gate, w_gate,
      b_gate, b_gate, b_gate, b_gate, w_branch)


def _outproj_kernel(m_ref, x_ref, mod_ref, wout_ref, o_ref):
    proj = jnp.dot(m_ref[...], wout_ref[...], preferred_element_type=F32)
    o_ref[...] = x_ref[...] + mod_ref[2:3, :] * proj


def _outproj(merged, x2, mod, w_out, seq):
    n, d = x2.shape
    tm = OUTPROJ_TM
    per_batch = seq // tm
    return pl.pallas_call(
        _outproj_kernel,
        grid=(n // tm,),
        in_specs=[
            pl.BlockSpec((tm, d), lambda i: (i, 0)),
            pl.BlockSpec((tm, d), lambda i: (i, 0)),
            pl.BlockSpec((None, 6, d), lambda i: (i // per_batch, 0, 0)),
            pl.BlockSpec((None, d, d), lambda i: (0, 0, 0), pipeline_mode=pl.Buffered(1)),
        ],
        out_specs=pl.BlockSpec((tm, d), lambda i: (i, 0)),
        out_shape=jax.ShapeDtypeStruct((n, d), F32),
        compiler_params=_params("arbitrary"),
        name="outproj",
    )(merged, x2, mod, w_out)


def _mlp_kernel(x_ref, mod_ref, gain_ref, w1_ref, w2_ref, o_ref, h_ref):
    j = pl.program_id(1)

    @pl.when(j == 0)
    def _():
        x = x_ref[...]
        h = _modulated_rmsnorm(x, gain_ref[...], mod_ref[3:4, :], mod_ref[4:5, :])
        h_ref[...] = h.astype(BF16)
        o_ref[...] = x

    hid = jnp.maximum(jnp.dot(h_ref[...], w1_ref[...], preferred_element_type=F32), 0.0)
    part = jnp.dot((hid * hid).astype(BF16), w2_ref[...], preferred_element_type=F32)
    o_ref[...] += mod_ref[5:6, :] * part


def _mlp(x2, mod, gain, w1, w2, seq):
    n, d = x2.shape
    tm, tf = MLP_TM, MLP_TF
    per_batch = seq // tm
    return pl.pallas_call(
        _mlp_kernel,
        grid=(n // tm, D_FF // tf),
        in_specs=[
            pl.BlockSpec((tm, d), lambda i, j: (i, 0)),
            pl.BlockSpec((None, 6, d), lambda i, j: (i // per_batch, 0, 0)),
            pl.BlockSpec((1, d), lambda i, j: (0, 0)),
            pl.BlockSpec((None, d, tf), lambda i, j: (0, 0, j)),
            pl.BlockSpec((None, tf, d), lambda i, j: (0, j, 0)),
        ],
        out_specs=pl.BlockSpec((tm, d), lambda i, j: (i, 0)),
        out_shape=jax.ShapeDtypeStruct((n, d), F32),
        scratch_shapes=[pltpu.VMEM((tm, d), BF16)],
        compiler_params=_params("arbitrary", "arbitrary"),
        name="mlp",
    )(x2, mod, gain, w1, w2)


def kernel(x, c, w_ada, b_ada, norm_gain, w_in, b_f, b_gate, q_gain, k_gain, conf_dw, conf_db,
           conf_ln_g, conf_ln_b, pool_w, pool_scale, sconv_w, w_branch, w_out, w_mlp1, w_mlp2):
    batch, seq, d = x.shape
    depth = w_ada.shape[0]
    W = WIDTH
    att_cols = 3 * W + HEADS

    w_t = jnp.swapaxes(w_in, 1, 2)
    w_mix = _repack_mixer_rows(w_t, 0)
    w_forget = jnp.pad(w_t[:, 3 * W:att_cols], ((0, 0), (0, LANES - HEADS), (0, 0))).astype(BF16)
    pool_w_b = pool_w.astype(BF16)
    b_gate3 = b_gate.reshape(depth, 1, 4 * d)

    mod_all = _ada(c, w_ada, b_ada).reshape(depth, batch, 6, d)
    x2 = x.reshape(batch * seq, d)
    for l in range(depth):
        mod = mod_all[l]
        b_f_row = jnp.pad(b_f[l], (0, LANES - HEADS)).reshape(1, LANES)

        z, zf, h = _inproj(x2, mod, norm_gain[l, 0:1], w_mix, w_forget, l, seq)
        y_rest = _conv_mixers(z, conf_dw[l], conf_db[l].reshape(1, W), conf_ln_g[l].reshape(1, W),
                              conf_ln_b[l].reshape(1, W), pool_w_b, l,
                              pool_scale[l].reshape(1, W), sconv_w[l], batch, seq)
        repack = [(w_t, l, MIX_COLS, GATE_COLS)]
        if l + 1 < depth:
            repack.append((w_t, l + 1, 0, MIX_COLS))
        y_att, *side = _attention(z, zf, b_f_row,
                                  jnp.tile(q_gain[l], HEADS_PER_STEP).reshape(1, LANES),
                                  jnp.tile(k_gain[l], HEADS_PER_STEP).reshape(1, LANES), batch, seq,
                                  cast=(w_mlp1, w_mlp2, w_out, w_branch.reshape(depth, 4 * W, d)),
                                  cast_layer=l, repack=repack)
        w_mlp1_b, w_mlp2_b, w_out_b, w_branch_b, w_gate = side[:5]
        w_mix = side[5] if l + 1 < depth else None
        w_branch_b = w_branch_b.reshape(1, 4, W, d)
        merged = _gated_merge(h, y_att, y_rest, w_gate, b_gate3, w_branch_b, l)
        x2 = _outproj(merged, x2, mod, w_out_b, seq)
        x2 = _mlp(x2, mod, norm_gain[l, 1:2], w_mlp1_b, w_mlp2_b, seq)
    return x2.reshape(batch, seq, d)
```
